```python
import math
import jax, jax.numpy as jnp
from jax import lax
import numpy as np

D_MODEL = 1024
BATCH = 8
SEQ = 4096
DEPTH = 1

CHUNK = 64
LEFT_CHUNKS = 8
BAND_CHUNKS = LEFT_CHUNKS + 1
BAND = BAND_CHUNKS * CHUNK

N_HEADS = 8
HEAD_DIM = 64
ATTN_WIDTH = N_HEADS * HEAD_DIM
MAX_REL = 128
N_REL = CHUNK + MAX_REL
ATTN_SCALE = 1.0 / math.sqrt(HEAD_DIM)
NEG_INF = -1e30

SSM_WIDTH = 512
SSM_GROUP = 16
SSM_GROUPS = SSM_WIDTH // SSM_GROUP
SSM_STATE = 64
DT_MIN = 0.001
DT_MAX = 0.1

NORM_EPS = 1e-6
IN_COLS = 4 * ATTN_WIDTH + 2 * SSM_WIDTH + 2 * D_MODEL
SPLITS = list(np.cumsum([ATTN_WIDTH, ATTN_WIDTH, ATTN_WIDTH, ATTN_WIDTH, SSM_WIDTH, SSM_WIDTH, D_MODEL])[:])

kernel_name = "hybrid_band_attn_s5_gated_block"


def rms_norm(x, gain):
    xf = x.astype(jnp.float32)
    inv = lax.rsqrt(jnp.mean(xf * xf, axis=-1, keepdims=True) + NORM_EPS)
    return (xf * inv * gain.astype(jnp.float32)).astype(x.dtype)


def band_attention(q, k, v, rel_bias):
    b, l, h, dh = q.shape
    nc = l // CHUNK
    qc = q.reshape(b, nc, CHUNK, h, dh)
    kc = k.reshape(b, nc, CHUNK, h, dh)
    vc = v.reshape(b, nc, CHUNK, h, dh)
    pad = ((0, 0), (LEFT_CHUNKS, 0), (0, 0), (0, 0), (0, 0))
    kp = jnp.pad(kc, pad)
    vp = jnp.pad(vc, pad)
    k_band = jnp.stack([kp[:, j:j + nc] for j in range(BAND_CHUNKS)], axis=2).reshape(b, nc, BAND, h, dh)
    v_band = jnp.stack([vp[:, j:j + nc] for j in range(BAND_CHUNKS)], axis=2).reshape(b, nc, BAND, h, dh)
    scores = jnp.einsum("bnqhd,bnkhd->bnhqk", qc, k_band,
                        preferred_element_type=jnp.float32) * ATTN_SCALE
    qi = jnp.arange(CHUNK)[:, None] + LEFT_CHUNKS * CHUNK
    kp_idx = jnp.arange(BAND)[None, :]
    rel = jnp.clip(qi - kp_idx, -(CHUNK - 1), MAX_REL) + (CHUNK - 1)
    bias = rel_bias.astype(jnp.float32)[:, rel]
    key_chunk = jnp.arange(nc)[:, None] - LEFT_CHUNKS + (jnp.arange(BAND) // CHUNK)[None, :]
    valid = (key_chunk >= 0)[None, :, None, None, :]
    scores = jnp.where(valid, scores + bias[None, None], NEG_INF)
    probs = jax.nn.softmax(scores, axis=-1).astype(v.dtype)
    out = jnp.einsum("bnhqk,bnkhd->bnqhd", probs, v_band)
    return out.reshape(b, l, h * dh)


def _complex_linear_combine(e1, e2):
    a1r, a1i, b1r, b1i = e1
    a2r, a2i, b2r, b2i = e2
    ar = a2r * a1r - a2i * a1i
    ai = a2r * a1i + a2i * a1r
    br = a2r * b1r - a2i * b1i + b2r
    bi = a2r * b1i + a2i * b1r + b2i
    return ar, ai, br, bi


def s5_ssm(u, a_re, a_im, log_dt, b_re, b_im, c_re, c_im, d_skip):
    bsz, l, _ = u.shape
    f32 = jnp.float32
    uf = u.astype(f32).reshape(bsz, l, SSM_GROUPS, SSM_GROUP)
    lam_re = a_re.astype(f32)
    lam_im = a_im.astype(f32)
    dt = jnp.exp(log_dt.astype(f32))[:, None]
    mag = jnp.exp(lam_re * dt)
    ab_re = mag * jnp.cos(lam_im * dt)
    ab_im = mag * jnp.sin(lam_im * dt)
    n_re = ab_re - 1.0
    n_im = ab_im
    den = lam_re * lam_re + lam_im * lam_im
    f_re = ((n_re * lam_re + n_im * lam_im) / den)[..., None]
    f_im = ((n_im * lam_re - n_re * lam_im) / den)[..., None]
    br_ = b_re.astype(f32)
    bi_ = b_im.astype(f32)
    bb_re = f_re * br_ - f_im * bi_
    bb_im = f_re * bi_ + f_im * br_
    bu_re = jnp.einsum("blgc,gpc->blgp", uf, bb_re)
    bu_im = jnp.einsum("blgc,gpc->blgp", uf, bb_im)
    a_r = jnp.broadcast_to(ab_re, bu_re.shape)
    a_i = jnp.broadcast_to(ab_im, bu_im.shape)
    _, _, h_re, h_im = lax.associative_scan(_complex_linear_combine, (a_r, a_i, bu_re, bu_im), axis=1)
    y = (jnp.einsum("blgp,gcp->blgc", h_re, c_re.astype(f32))
         - jnp.einsum("blgp,gcp->blgc", h_im, c_im.astype(f32)))
    y = y.reshape(bsz, l, SSM_WIDTH) + d_skip.astype(f32) * uf.reshape(bsz, l, SSM_WIDTH)
    return y.astype(u.dtype)


def setup_inputs(seed: int = 0) -> dict:
    key = jax.random.key(seed)
    ks = jax.random.split(key, 20)
    f32 = jnp.float32
    x = jax.random.normal(ks[0], (BATCH, SEQ, D_MODEL), f32)
    norm_gain = 1.0 + 0.02 * jax.random.normal(ks[1], (DEPTH, D_MODEL), f32)
    w_in = jax.random.normal(ks[2], (DEPTH, D_MODEL, IN_COLS), f32) * D_MODEL ** -0.5
    rel_bias = 0.1 * jax.random.normal(ks[3], (DEPTH, N_HEADS, N_REL), f32)
    n_idx = jnp.arange(SSM_STATE, dtype=f32)[None, None, :]
    ssm_a_re = -0.5 + 0.01 * jax.random.normal(ks[4], (DEPTH, SSM_GROUPS, SSM_STATE), f32)
    ssm_a_im = math.pi * n_idx + 0.01 * jax.random.normal(ks[5], (DEPTH, SSM_GROUPS, SSM_STATE), f32)
    ssm_log_dt = jax.random.uniform(ks[6], (DEPTH, SSM_GROUPS), f32,
                                    minval=math.log(DT_MIN), maxval=math.log(DT_MAX))
    b_scale = (2.0 * SSM_GROUP) ** -0.5
    ssm_b_re = jax.random.normal(ks[7], (DEPTH, SSM_GROUPS, SSM_STATE, SSM_GROUP), f32) * b_scale
    ssm_b_im = jax.random.normal(ks[8], (DEPTH, SSM_GROUPS, SSM_STATE, SSM_GROUP), f32) * b_scale
    c_scale = (2.0 * SSM_STATE) ** -0.5
    ssm_c_re = jax.random.normal(ks[9], (DEPTH, SSM_GROUPS, SSM_GROUP, SSM_STATE), f32) * c_scale
    ssm_c_im = jax.random.normal(ks[10], (DEPTH, SSM_GROUPS, SSM_GROUP, SSM_STATE), f32) * c_scale
    ssm_d = jax.random.normal(ks[11], (DEPTH, SSM_WIDTH), f32)
    w_glu = jax.random.normal(ks[12], (DEPTH, SSM_WIDTH, 2 * SSM_WIDTH), f32) * SSM_WIDTH ** -0.5
    b_glu = 0.01 * jax.random.normal(ks[13], (DEPTH, 2 * SSM_WIDTH), f32)
    w_attn_out = jax.random.normal(ks[14], (DEPTH, ATTN_WIDTH, D_MODEL), f32) * ATTN_WIDTH ** -0.5
    w_ssm_out = jax.random.normal(ks[15], (DEPTH, SSM_WIDTH, D_MODEL), f32) * SSM_WIDTH ** -0.5
    gate_bias = 0.01 * jax.random.normal(ks[16], (DEPTH, 2 * D_MODEL), f32)
    w_out = jax.random.normal(ks[17], (DEPTH, D_MODEL, D_MODEL), f32) * D_MODEL ** -0.5
    final_gain = 1.0 + 0.02 * jax.random.normal(ks[18], (D_MODEL,), f32)
    return {"x": x, "norm_gain": norm_gain, "w_in": w_in, "rel_bias": rel_bias,
            "ssm_a_re": ssm_a_re, "ssm_a_im": ssm_a_im, "ssm_log_dt": ssm_log_dt,
            "ssm_b_re": ssm_b_re, "ssm_b_im": ssm_b_im, "ssm_c_re": ssm_c_re, "ssm_c_im": ssm_c_im,
            "ssm_d": ssm_d, "w_glu": w_glu, "b_glu": b_glu, "w_attn_out": w_attn_out,
            "w_ssm_out": w_ssm_out, "gate_bias": gate_bias, "w_out": w_out, "final_gain": final_gain}


def reference(x, norm_gain, w_in, rel_bias, ssm_a_re, ssm_a_im, ssm_log_dt, ssm_b_re, ssm_b_im,
              ssm_c_re, ssm_c_im, ssm_d, w_glu, b_glu, w_attn_out, w_ssm_out, gate_bias, w_out,
              final_gain):
    bsz, l, _ = x.shape
    for layer in range(DEPTH):
        h = rms_norm(x, norm_gain[layer])
        proj = h @ w_in[layer]
        q, k, v, z_a, u, z_s, g_a, g_s = jnp.split(proj, SPLITS, axis=-1)
        g_a = g_a + gate_bias[layer, :D_MODEL]
        g_s = g_s + gate_bias[layer, D_MODEL:]
        heads = lambda t: t.reshape(bsz, l, N_HEADS, HEAD_DIM)
        y_a = band_attention(heads(q), heads(k), heads(v), rel_bias[layer])
        y_a = (y_a * jax.nn.silu(z_a)) @ w_attn_out[layer]
        y_s = s5_ssm(u, ssm_a_re[layer], ssm_a_im[layer], ssm_log_dt[layer], ssm_b_re[layer],
                     ssm_b_im[layer], ssm_c_re[layer], ssm_c_im[layer], ssm_d[layer])
        y_s = jax.nn.gelu(y_s)
        glu_a, glu_b = jnp.split(y_s @ w_glu[layer] + b_glu[layer], 2, axis=-1)
        y_s = glu_a * jax.nn.sigmoid(glu_b)
        y_s = (y_s * jax.nn.silu(z_s)) @ w_ssm_out[layer]
        merged = jax.nn.sigmoid(g_a) * y_a + jax.nn.sigmoid(g_s) * y_s
        x = x + merged @ w_out[layer]
    return rms_norm(x, final_gain)
```

```python
import functools
import math

import jax
import jax.numpy as jnp
from jax import lax
from jax.experimental import pallas as pl
from jax.experimental.pallas import tpu as pltpu

F32 = jnp.float32
BF16 = jnp.bfloat16

CHUNK = 64
LEFT_CHUNKS = 8
BAND = (LEFT_CHUNKS + 1) * CHUNK
HIST = LEFT_CHUNKS * CHUNK
N_HEADS = 8
HEAD_DIM = 64
ATTN_WIDTH = N_HEADS * HEAD_DIM
MAX_REL = 128
SSM_WIDTH = 512
SSM_GROUP = 16
SSM_GROUPS = SSM_WIDTH // SSM_GROUP
SSM_STATE = 64
N_STATES = SSM_GROUPS * SSM_STATE
NORM_EPS = 1e-6
NEG_INF = -1e30

VMEM_LIMIT_BYTES = 56 * 1024 * 1024


def _rms_norm(x, gain):
    inv = lax.rsqrt(jnp.mean(x * x, axis=-1, keepdims=True) + NORM_EPS)
    return x * inv * gain


def _const_spec(shape):
    zeros = (0,) * len(shape)
    return pl.BlockSpec(shape, lambda *_: zeros)


def _uproj_kernel(x_ref, gain_ref, w_ref, u_ref):
    h = _rms_norm(x_ref[...], gain_ref[...]).astype(BF16)
    u_ref[...] = jnp.dot(h, w_ref[...], preferred_element_type=F32)


def _uproj(x, gain, w_u, tm):
    b, l, d = x.shape
    return pl.pallas_call(
        _uproj_kernel,
        grid=(b, l // tm),
        in_specs=[
            pl.BlockSpec((None, tm, d), lambda bi, i: (bi, i, 0)),
            _const_spec((1, d)),
            _const_spec((d, SSM_WIDTH)),
        ],
        out_specs=pl.BlockSpec((None, tm, SSM_WIDTH), lambda bi, i: (bi, i, 0)),
        out_shape=jax.ShapeDtypeStruct((b, l, SSM_WIDTH), F32),
        compiler_params=pltpu.CompilerParams(
            dimension_semantics=("arbitrary", "arbitrary"), vmem_limit_bytes=VMEM_LIMIT_BYTES),
        name="uproj",
    )(x, gain, w_u)


def _ssm_kernel(u_ref, bre_ref, bim_ref, are_ref, aim_ref, cre_ref, cim_ref, d_ref, wglu_ref,
                bglu_ref, s_ref, hre_s, him_s, st_re, st_im, *, steps, batch):
    i = pl.program_id(0)

    @pl.when(i == 0)
    def _():
        st_re[...] = jnp.zeros_like(st_re)
        st_im[...] = jnp.zeros_like(st_im)

    u = u_ref[...]
    ub = u.astype(BF16)
    half_in = SSM_WIDTH // 2
    half_st = N_STATES // 2
    for hf in range(2):
        uh = ub[:, half_in * hf:half_in * (hf + 1)]
        hre_s[:, half_st * hf:half_st * (hf + 1)] = jnp.dot(uh, bre_ref[hf], preferred_element_type=F32)
        him_s[:, half_st * hf:half_st * (hf + 1)] = jnp.dot(uh, bim_ref[hf], preferred_element_type=F32)

    lanes = 512
    for q in range(N_STATES // lanes):
        sl = slice(lanes * q, lanes * (q + 1))
        ar = jnp.broadcast_to(are_ref[:, sl], (batch, lanes))
        ai = jnp.broadcast_to(aim_ref[:, sl], (batch, lanes))

        def body(t, carry, sl=sl, ar=ar, ai=ai):
            hr, hi = carry
            r0 = pl.multiple_of(t * batch, batch)
            nr = ar * hr - ai * hi + hre_s[pl.ds(r0, batch), sl]
            ni = ar * hi + ai * hr + him_s[pl.ds(r0, batch), sl]
            hre_s[pl.ds(r0, batch), sl] = nr
            him_s[pl.ds(r0, batch), sl] = ni
            return nr, ni

        hr, hi = lax.fori_loop(0, steps, body, (st_re[:, sl], st_im[:, sl]), unroll=4)
        st_re[:, sl] = hr
        st_im[:, sl] = hi

    ys = []
    for hf in range(2):
        hr = hre_s[:, half_st * hf:half_st * (hf + 1)].astype(BF16)
        hi = him_s[:, half_st * hf:half_st * (hf + 1)].astype(BF16)
        y = (jnp.dot(hr, cre_ref[hf], preferred_element_type=F32)
             - jnp.dot(hi, cim_ref[hf], preferred_element_type=F32))
        cs = slice(half_in * hf, half_in * (hf + 1))
        ys.append(y + d_ref[:, cs] * u[:, cs])
    y = jnp.concatenate(ys, axis=1)
    g = jax.nn.gelu(y)
    z = jnp.dot(g.astype(BF16), wglu_ref[...], preferred_element_type=F32) + bglu_ref[...]
    s_ref[...] = z[:, :SSM_WIDTH] * jax.nn.sigmoid(z[:, SSM_WIDTH:])


def _ssm(u_tm, bre, bim, are, aim, cre, cim, d_skip, w_glu, b_glu, steps, batch):
    rows_total = u_tm.shape[0]
    rows = steps * batch
    kern = functools.partial(_ssm_kernel, steps=steps, batch=batch)
    return pl.pallas_call(
        kern,
        grid=(rows_total // rows,),
        in_specs=[
            pl.BlockSpec((rows, SSM_WIDTH), lambda i: (i, 0)),
            _const_spec(bre.shape), _const_spec(bim.shape),
            _const_spec(are.shape), _const_spec(aim.shape),
            _const_spec(cre.shape), _const_spec(cim.shape),
            _const_spec(d_skip.shape), _const_spec(w_glu.shape), _const_spec(b_glu.shape),
        ],
        out_specs=pl.BlockSpec((rows, SSM_WIDTH), lambda i: (i, 0)),
        out_shape=jax.ShapeDtypeStruct((rows_total, SSM_WIDTH), F32),
        scratch_shapes=[
            pltpu.VMEM((rows, N_STATES), F32), pltpu.VMEM((rows, N_STATES), F32),
            pltpu.VMEM((batch, N_STATES), F32), pltpu.VMEM((batch, N_STATES), F32),
        ],
        compiler_params=pltpu.CompilerParams(
            dimension_semantics=("arbitrary",), vmem_limit_bytes=VMEM_LIMIT_BYTES),
        name="ssm",
    )(u_tm, bre, bim, are, aim, cre, cim, d_skip, w_glu, b_glu)


def _main_kernel(x_ref, s_ref, gain_ref, wqkv_ref, wz_ref, wg_ref, gb_ref, bias_ref, wao_ref,
                 wso_ref, wo_ref, fgain_ref, o_ref, kbuf, vbuf, q_s, ya_s, *, tm):
    i = pl.program_id(1)
    n_chunks = tm // CHUNK
    x = x_ref[...]
    h = _rms_norm(x, gain_ref[...]).astype(BF16)

    qkv = jnp.dot(h, wqkv_ref[...], preferred_element_type=F32)
    q_s[...] = qkv[:, :ATTN_WIDTH].astype(BF16)

    @pl.when(i == 0)
    def _():
        kbuf[0:HIST, :] = jnp.zeros((HIST, ATTN_WIDTH), BF16)
        vbuf[0:HIST, :] = jnp.zeros((HIST, ATTN_WIDTH), BF16)

    @pl.when(i > 0)
    def _():
        kbuf[0:HIST, :] = kbuf[tm:tm + HIST, :]
        vbuf[0:HIST, :] = vbuf[tm:tm + HIST, :]

    kbuf[HIST:HIST + tm, :] = qkv[:, ATTN_WIDTH:2 * ATTN_WIDTH].astype(BF16)
    vbuf[HIST:HIST + tm, :] = qkv[:, 2 * ATTN_WIDTH:].astype(BF16)

    col = lax.broadcasted_iota(jnp.int32, (CHUNK, BAND), 1)

    def chunk_body(c, carry):
        r0 = pl.multiple_of(c * CHUNK, CHUNK)
        first_valid = jnp.maximum(LEFT_CHUNKS - (i * n_chunks + c), 0) * CHUNK
        valid = col >= first_valid
        qc = q_s[pl.ds(r0, CHUNK), :]
        kb = kbuf[pl.ds(r0, BAND), :]
        vb = vbuf[pl.ds(r0, BAND), :]
        for hd in range(N_HEADS):
            hs = slice(HEAD_DIM * hd, HEAD_DIM * (hd + 1))
            s = lax.dot_general(qc[:, hs], kb[:, hs], (((1,), (1,)), ((), ())),
                                preferred_element_type=F32)
            s = jnp.where(valid, s + bias_ref[hd], NEG_INF)
            m = jnp.max(s, axis=-1, keepdims=True)
            p = jnp.exp(s - m)
            denom = jnp.sum(p, axis=-1, keepdims=True)
            pv = jnp.dot(p.astype(BF16), vb[:, hs], preferred_element_type=F32)
            ya_s[pl.ds(r0, CHUNK), hs] = pv / denom
        return carry

    lax.fori_loop(0, n_chunks, chunk_body, 0)

    z = jnp.dot(h, wz_ref[...], preferred_element_type=F32)
    ya = ya_s[...] * jax.nn.silu(z[:, :ATTN_WIDTH])
    yao = jnp.dot(ya.astype(BF16), wao_ref[...], preferred_element_type=F32)
    ys = s_ref[...] * jax.nn.silu(z[:, ATTN_WIDTH:])
    yso = jnp.dot(ys.astype(BF16), wso_ref[...], preferred_element_type=F32)
    g = jnp.dot(h, wg_ref[...], preferred_element_type=F32) + gb_ref[...]
    d = x.shape[-1]
    merged = jax.nn.sigmoid(g[:, :d]) * yao + jax.nn.sigmoid(g[:, d:]) * yso
    xn = x + jnp.dot(merged.astype(BF16), wo_ref[...], preferred_element_type=F32)
    o_ref[...] = _rms_norm(xn, fgain_ref[...])


def _main(x, s, gain, wqkv, wz, wg, gb, bias, wao, wso, wo, fgain, tm):
    b, l, d = x.shape
    kern = functools.partial(_main_kernel, tm=tm)
    consts = [gain, wqkv, wz, wg, gb, bias, wao, wso, wo, fgain]
    return pl.pallas_call(
        kern,
        grid=(b, l // tm),
        in_specs=[
            pl.BlockSpec((None, tm, d), lambda bi, i: (bi, i, 0)),
            pl.BlockSpec((None, tm, SSM_WIDTH), lambda bi, i: (bi, i, 0)),
        ] + [_const_spec(c.shape) for c in consts],
        out_specs=pl.BlockSpec((None, tm, d), lambda bi, i: (bi, i, 0)),
        out_shape=jax.ShapeDtypeStruct((b, l, d), F32),
        scratch_shapes=[
            pltpu.VMEM((HIST + tm, ATTN_WIDTH), BF16), pltpu.VMEM((HIST + tm, ATTN_WIDTH), BF16),
            pltpu.VMEM((tm, ATTN_WIDTH), BF16), pltpu.VMEM((tm, ATTN_WIDTH), F32),
        ],
        compiler_params=pltpu.CompilerParams(
            dimension_semantics=("arbitrary", "arbitrary"), vmem_limit_bytes=VMEM_LIMIT_BYTES),
        name="main_block",
    )(x, s, *consts)


def _block_diag_halves(blocks):
    g, r, c = blocks.shape
    hg = g // 2
    eye = jnp.eye(hg, dtype=blocks.dtype)
    bl = blocks.reshape(2, hg, r, c)
    out = bl[:, :, :, None, :] * eye[None, :, None, :, None]
    return out.reshape(2, hg * r, hg * c)


def _ssm_params(a_re, a_im, log_dt, b_re, b_im, c_re, c_im):
    dt = jnp.exp(log_dt)[:, None]
    mag = jnp.exp(a_re * dt)
    ab_re = mag * jnp.cos(a_im * dt)
    ab_im = mag * jnp.sin(a_im * dt)
    n_re = ab_re - 1.0
    n_im = ab_im
    den = a_re * a_re + a_im * a_im
    f_re = ((n_re * a_re + n_im * a_im) / den)[..., None]
    f_im = ((n_im * a_re - n_re * a_im) / den)[..., None]
    bb_re = f_re * b_re - f_im * b_im
    bb_im = f_re * b_im + f_im * b_re
    bre = _block_diag_halves(jnp.swapaxes(bb_re, 1, 2)).astype(BF16)
    bim = _block_diag_halves(jnp.swapaxes(bb_im, 1, 2)).astype(BF16)
    cre = _block_diag_halves(jnp.swapaxes(c_re, 1, 2)).astype(BF16)
    cim = _block_diag_halves(jnp.swapaxes(c_im, 1, 2)).astype(BF16)
    return bre, bim, ab_re.reshape(1, -1), ab_im.reshape(1, -1), cre, cim


def _rel_bias_table(rel_bias):
    qi = jnp.arange(CHUNK)[:, None] + LEFT_CHUNKS * CHUNK
    kp = jnp.arange(BAND)[None, :]
    rel = jnp.clip(qi - kp, -(CHUNK - 1), MAX_REL) + (CHUNK - 1)
    return rel_bias[:, rel]


def kernel(x, norm_gain, w_in, rel_bias, ssm_a_re, ssm_a_im, ssm_log_dt, ssm_b_re, ssm_b_im,
           ssm_c_re, ssm_c_im, ssm_d, w_glu, b_glu, w_attn_out, w_ssm_out, gate_bias, w_out,
           final_gain):
    assert norm_gain.shape[0] == 1, "single-layer block"
    b, l, d = x.shape
    aw, sw = ATTN_WIDTH, SSM_WIDTH
    w = w_in[0]
    scale = 1.0 / math.sqrt(HEAD_DIM)
    wqkv = jnp.concatenate([w[:, :aw] * scale, w[:, aw:3 * aw]], axis=1).astype(BF16)
    wz = jnp.concatenate([w[:, 3 * aw:4 * aw], w[:, 4 * aw + sw:4 * aw + 2 * sw]], axis=1).astype(BF16)
    w_u = w[:, 4 * aw:4 * aw + sw].astype(BF16)
    wg = w[:, 4 * aw + 2 * sw:].astype(BF16)
    gain = norm_gain[0].reshape(1, d)

    u = _uproj(x, gain, w_u, tm=512)
    u_tm = jnp.transpose(u, (1, 0, 2)).reshape(l * b, sw)

    bre, bim, are, aim, cre, cim = _ssm_params(
        ssm_a_re[0], ssm_a_im[0], ssm_log_dt[0], ssm_b_re[0], ssm_b_im[0], ssm_c_re[0], ssm_c_im[0])
    s_tm = _ssm(u_tm, bre, bim, are, aim, cre, cim, ssm_d[0].reshape(1, sw),
                w_glu[0].astype(BF16), b_glu[0].reshape(1, 2 * sw), steps=64, batch=b)
    s = jnp.transpose(s_tm.reshape(l, b, sw), (1, 0, 2))

    return _main(x, s, gain, wqkv, wz, wg, gate_bias[0].reshape(1, 2 * d),
                 _rel_bias_table(rel_bias[0]), w_attn_out[0].astype(BF16),
                 w_ssm_out[0].astype(BF16), w_out[0].astype(BF16), final_gain.reshape(1, d),
                 tm=256)
```

```python
import functools
import math

import jax
import jax.numpy as jnp
from jax import lax
from jax.experimental import pallas as pl
from jax.experimental.pallas import tpu as pltpu

F32 = jnp.float32
BF16 = jnp.bfloat16

CHUNK = 64
LEFT_CHUNKS = 8
BAND = (LEFT_CHUNKS + 1) * CHUNK
HIST = LEFT_CHUNKS * CHUNK
N_HEADS = 8
HEAD_DIM = 64
ATTN_WIDTH = N_HEADS * HEAD_DIM
MAX_REL = 128
SSM_WIDTH = 512
SSM_GROUP = 16
SSM_GROUPS = SSM_WIDTH // SSM_GROUP
SSM_STATE = 64
N_STATES = SSM_GROUPS * SSM_STATE
NORM_EPS = 1e-6
NEG_INF = -1e30
LOG2E = math.log2(math.e)

VMEM_LIMIT_BYTES = 56 * 1024 * 1024


def _rms_norm(x, gain):
    inv = lax.rsqrt(jnp.mean(x * x, axis=-1, keepdims=True) + NORM_EPS)
    return x * inv * gain


def _const_spec(shape):
    zeros = (0,) * len(shape)
    return pl.BlockSpec(shape, lambda *_: zeros, pipeline_mode=pl.Buffered(1))


def _uproj_kernel(x_ref, gain_ref, w_ref, u_ref):
    h = _rms_norm(x_ref[...], gain_ref[...]).astype(BF16)
    u_ref[...] = jnp.dot(h, w_ref[...], preferred_element_type=F32)


def _uproj(x, gain, w_u, tm):
    b, l, d = x.shape
    return pl.pallas_call(
        _uproj_kernel,
        grid=(b, l // tm),
        in_specs=[
            pl.BlockSpec((None, tm, d), lambda bi, i: (bi, i, 0)),
            _const_spec((1, d)),
            _const_spec((d, SSM_WIDTH)),
        ],
        out_specs=pl.BlockSpec((None, tm, SSM_WIDTH), lambda bi, i: (bi, i, 0)),
        out_shape=jax.ShapeDtypeStruct((b, l, SSM_WIDTH), F32),
        compiler_params=pltpu.CompilerParams(
            dimension_semantics=("arbitrary", "arbitrary"), vmem_limit_bytes=VMEM_LIMIT_BYTES),
        name="uproj",
    )(x, gain, w_u)


def _ssm_kernel(u_ref, bre_ref, bim_ref, are_ref, aim_ref, cre_ref, cim_ref, d_ref, wglu_ref,
                bglu_ref, s_ref, hre_s, him_s, st_re, st_im, *, steps, batch):
    i = pl.program_id(0)

    @pl.when(i == 0)
    def _():
        st_re[...] = jnp.zeros_like(st_re)
        st_im[...] = jnp.zeros_like(st_im)

    u = u_ref[...]
    ub = u.astype(BF16)
    half_in = SSM_WIDTH // 2
    half_st = N_STATES // 2
    for hf in range(2):
        uh = ub[:, half_in * hf:half_in * (hf + 1)]
        hre_s[:, half_st * hf:half_st * (hf + 1)] = jnp.dot(uh, bre_ref[hf], preferred_element_type=F32)
        him_s[:, half_st * hf:half_st * (hf + 1)] = jnp.dot(uh, bim_ref[hf], preferred_element_type=F32)

    lanes = 512
    for q in range(N_STATES // lanes):
        sl = slice(lanes * q, lanes * (q + 1))
        ar = jnp.broadcast_to(are_ref[:, sl], (batch, lanes))
        ai = jnp.broadcast_to(aim_ref[:, sl], (batch, lanes))

        def body(t, carry, sl=sl, ar=ar, ai=ai):
            hr, hi = carry
            r0 = pl.multiple_of(t * batch, batch)
            nr = ar * hr - ai * hi + hre_s[pl.ds(r0, batch), sl]
            ni = ar * hi + ai * hr + him_s[pl.ds(r0, batch), sl]
            hre_s[pl.ds(r0, batch), sl] = nr
            him_s[pl.ds(r0, batch), sl] = ni
            return nr, ni

        hr, hi = lax.fori_loop(0, steps, body, (st_re[:, sl], st_im[:, sl]), unroll=4)
        st_re[:, sl] = hr
        st_im[:, sl] = hi

    ys = []
    for hf in range(2):
        hr = hre_s[:, half_st * hf:half_st * (hf + 1)].astype(BF16)
        hi = him_s[:, half_st * hf:half_st * (hf + 1)].astype(BF16)
        y = (jnp.dot(hr, cre_ref[hf], preferred_element_type=F32)
             - jnp.dot(hi, cim_ref[hf], preferred_element_type=F32))
        cs = slice(half_in * hf, half_in * (hf + 1))
        ys.append(y + d_ref[:, cs] * u[:, cs])
    y = jnp.concatenate(ys, axis=1)
    g = jax.nn.gelu(y)
    z = jnp.dot(g.astype(BF16), wglu_ref[...], preferred_element_type=F32) + bglu_ref[...]
    s_ref[...] = z[:, :SSM_WIDTH] * jax.nn.sigmoid(z[:, SSM_WIDTH:])


def _ssm(u_tm, bre, bim, are, aim, cre, cim, d_skip, w_glu, b_glu, steps, batch):
    rows_total = u_tm.shape[0]
    rows = steps * batch
    kern = functools.partial(_ssm_kernel, steps=steps, batch=batch)
    return pl.pallas_call(
        kern,
        grid=(rows_total // rows,),
        in_specs=[
            pl.BlockSpec((rows, SSM_WIDTH), lambda i: (i, 0)),
            _const_spec(bre.shape), _const_spec(bim.shape),
            _const_spec(are.shape), _const_spec(aim.shape),
            _const_spec(cre.shape), _const_spec(cim.shape),
            _const_spec(d_skip.shape), _const_spec(w_glu.shape), _const_spec(b_glu.shape),
        ],
        out_specs=pl.BlockSpec((rows, SSM_WIDTH), lambda i: (i, 0)),
        out_shape=jax.ShapeDtypeStruct((rows_total, SSM_WIDTH), F32),
        scratch_shapes=[
            pltpu.VMEM((rows, N_STATES), F32), pltpu.VMEM((rows, N_STATES), F32),
            pltpu.VMEM((batch, N_STATES), F32), pltpu.VMEM((batch, N_STATES), F32),
        ],
        compiler_params=pltpu.CompilerParams(
            dimension_semantics=("arbitrary",), vmem_limit_bytes=VMEM_LIMIT_BYTES),
        name="ssm",
    )(u_tm, bre, bim, are, aim, cre, cim, d_skip, w_glu, b_glu)


PAIR = 2 * CHUNK
WINDOW = BAND + CHUNK


def _attn_unit(q_s, kt_s, v_s, bias_ref, ya_s, m, pr, first_valid):
    rows = slice(PAIR * m, PAIR * (m + 1))
    cols = slice(PAIR * pr, PAIR * (pr + 1))
    w0 = PAIR * m + first_valid
    wlen = WINDOW - first_valid
    qp = q_s[rows, cols]
    lane = lax.broadcasted_iota(jnp.int32, (PAIR, PAIR), 1)
    zero = jnp.zeros_like(qp)
    lhs = jnp.concatenate([jnp.where(lane < HEAD_DIM, qp, zero),
                           jnp.where(lane >= HEAD_DIM, qp, zero)], axis=0)
    s = jnp.dot(lhs, kt_s[cols, w0:w0 + wlen], preferred_element_type=F32)
    s = s + bias_ref[pr, :, first_valid:WINDOW]
    p = jnp.exp2(s - jnp.max(s, axis=-1, keepdims=True))
    denom = jnp.sum(p, axis=-1, keepdims=True)
    o = jnp.dot(p.astype(BF16), v_s[w0:w0 + wlen, cols], preferred_element_type=F32)
    o = o / denom
    ya_s[rows, cols] = jnp.where(lane < HEAD_DIM, o[:PAIR], o[PAIR:])


def _main_kernel(x_ref, s_ref, gain_ref, wq_ref, wkt_ref, wv_ref, wz_ref, wg_ref, gb_ref,
                 bias_ref, wao_ref, wso_ref, wo_ref, fgain_ref, o_ref, kt_s, v_s, q_s, ya_s, *, tm):
    i = pl.program_id(1)
    x = x_ref[...]
    h = _rms_norm(x, gain_ref[...]).astype(BF16)

    q_s[...] = jnp.dot(h, wq_ref[...], preferred_element_type=F32).astype(BF16)
    kt_new = lax.dot_general(wkt_ref[...], h, (((1,), (1,)), ((), ())), preferred_element_type=F32)
    v_new = jnp.dot(h, wv_ref[...], preferred_element_type=F32)

    @pl.when(i > 0)
    def _():
        kt_s[:, 0:HIST] = kt_s[:, tm:tm + HIST]
        v_s[0:HIST, :] = v_s[tm:tm + HIST, :]

    kt_s[:, HIST:HIST + tm] = kt_new.astype(BF16)
    v_s[HIST:HIST + tm, :] = v_new.astype(BF16)

    def attention(masked_start):
        for m in range(tm // PAIR):
            first_valid = max(HIST - PAIR * m, 0) if masked_start else 0
            for pr in range(N_HEADS // 2):
                _attn_unit(q_s, kt_s, v_s, bias_ref, ya_s, m, pr, first_valid)

    @pl.when(i == 0)
    def _():
        attention(True)

    @pl.when(i > 0)
    def _():
        attention(False)

    z = jnp.dot(h, wz_ref[...], preferred_element_type=F32)
    ya = ya_s[...] * jax.nn.silu(z[:, :ATTN_WIDTH])
    yao = jnp.dot(ya.astype(BF16), wao_ref[...], preferred_element_type=F32)
    ys = s_ref[...] * jax.nn.silu(z[:, ATTN_WIDTH:])
    yso = jnp.dot(ys.astype(BF16), wso_ref[...], preferred_element_type=F32)
    g = jnp.dot(h, wg_ref[...], preferred_element_type=F32) + gb_ref[...]
    d = x.shape[-1]
    merged = jax.nn.sigmoid(g[:, :d]) * yao + jax.nn.sigmoid(g[:, d:]) * yso
    xn = x + jnp.dot(merged.astype(BF16), wo_ref[...], preferred_element_type=F32)
    o_ref[...] = _rms_norm(xn, fgain_ref[...])


def _main(x, s, consts):
    b, l, d = x.shape
    tm = HIST
    kern = functools.partial(_main_kernel, tm=tm)
    return pl.pallas_call(
        kern,
        grid=(b, l // tm),
        in_specs=[
            pl.BlockSpec((None, tm, d), lambda bi, i: (bi, i, 0)),
            pl.BlockSpec((None, tm, SSM_WIDTH), lambda bi, i: (bi, i, 0)),
        ] + [_const_spec(c.shape) for c in consts],
        out_specs=pl.BlockSpec((None, tm, d), lambda bi, i: (bi, i, 0)),
        out_shape=jax.ShapeDtypeStruct((b, l, d), F32),
        scratch_shapes=[
            pltpu.VMEM((ATTN_WIDTH, HIST + tm), BF16),
            pltpu.VMEM((HIST + tm, ATTN_WIDTH), BF16),
            pltpu.VMEM((tm, ATTN_WIDTH), BF16), pltpu.VMEM((tm, ATTN_WIDTH), F32),
        ],
        compiler_params=pltpu.CompilerParams(
            dimension_semantics=("arbitrary", "arbitrary"), vmem_limit_bytes=VMEM_LIMIT_BYTES),
        name="main_block",
    )(x, s, *consts)


def _block_diag_halves(blocks):
    g, r, c = blocks.shape
    hg = g // 2
    eye = jnp.eye(hg, dtype=blocks.dtype)
    bl = blocks.reshape(2, hg, r, c)
    out = bl[:, :, :, None, :] * eye[None, :, None, :, None]
    return out.reshape(2, hg * r, hg * c)


def _ssm_params(a_re, a_im, log_dt, b_re, b_im, c_re, c_im):
    dt = jnp.exp(log_dt)[:, None]
    mag = jnp.exp(a_re * dt)
    ab_re = mag * jnp.cos(a_im * dt)
    ab_im = mag * jnp.sin(a_im * dt)
    n_re = ab_re - 1.0
    n_im = ab_im
    den = a_re * a_re + a_im * a_im
    f_re = ((n_re * a_re + n_im * a_im) / den)[..., None]
    f_im = ((n_im * a_re - n_re * a_im) / den)[..., None]
    bb_re = f_re * b_re - f_im * b_im
    bb_im = f_re * b_im + f_im * b_re
    bre = _block_diag_halves(jnp.swapaxes(bb_re, 1, 2)).astype(BF16)
    bim = _block_diag_halves(jnp.swapaxes(bb_im, 1, 2)).astype(BF16)
    cre = _block_diag_halves(jnp.swapaxes(c_re, 1, 2)).astype(BF16)
    cim = _block_diag_halves(jnp.swapaxes(c_im, 1, 2)).astype(BF16)
    return bre, bim, ab_re.reshape(1, -1), ab_im.reshape(1, -1), cre, cim


def _attn_bias_tables(rel_bias):
    n_rel = rel_bias.shape[1]
    ext = jnp.concatenate(
        [rel_bias, jnp.broadcast_to(rel_bias[:, -1:], (N_HEADS, CHUNK + BAND - 1 - n_rel))], axis=1)
    by_rev_k = jnp.stack([ext[:, q:q + BAND] for q in range(CHUNK)], axis=1)
    bias = by_rev_k[:, :, ::-1] * LOG2E
    masked = jnp.full((N_HEADS, CHUNK, CHUNK), NEG_INF, F32)
    even = jnp.concatenate([bias, masked], axis=2)
    odd = jnp.concatenate([masked, bias], axis=2)
    per_head = jnp.concatenate([even, odd], axis=1)
    return per_head.reshape(N_HEADS // 2, 4 * CHUNK, WINDOW)


def kernel(x, norm_gain, w_in, rel_bias, ssm_a_re, ssm_a_im, ssm_log_dt, ssm_b_re, ssm_b_im,
           ssm_c_re, ssm_c_im, ssm_d, w_glu, b_glu, w_attn_out, w_ssm_out, gate_bias, w_out,
           final_gain):
    assert norm_gain.shape[0] == 1, "single-layer block"
    b, l, d = x.shape
    aw, sw = ATTN_WIDTH, SSM_WIDTH
    w = w_in[0]
    wq = (w[:, :aw] * (LOG2E / math.sqrt(HEAD_DIM))).astype(BF16)
    wkt = w[:, aw:2 * aw].T.astype(BF16)
    wv = w[:, 2 * aw:3 * aw].astype(BF16)
    wz = jnp.concatenate([w[:, 3 * aw:4 * aw], w[:, 4 * aw + sw:4 * aw + 2 * sw]], axis=1).astype(BF16)
    w_u = w[:, 4 * aw:4 * aw + sw].astype(BF16)
    wg = w[:, 4 * aw + 2 * sw:].astype(BF16)
    gain = norm_gain[0].reshape(1, d)

    u = _uproj(x, gain, w_u, tm=512)
    u_tm = jnp.transpose(u, (1, 0, 2)).reshape(l * b, sw)

    bre, bim, are, aim, cre, cim = _ssm_params(
        ssm_a_re[0], ssm_a_im[0], ssm_log_dt[0], ssm_b_re[0], ssm_b_im[0], ssm_c_re[0], ssm_c_im[0])
    s_tm = _ssm(u_tm, bre, bim, are, aim, cre, cim, ssm_d[0].reshape(1, sw),
                w_glu[0].astype(BF16), b_glu[0].reshape(1, 2 * sw), steps=64, batch=b)
    s = jnp.transpose(s_tm.reshape(l, b, sw), (1, 0, 2))

    consts = [gain, wq, wkt, wv, wz, wg, gate_bias[0].reshape(1, 2 * d),
              _attn_bias_tables(rel_bias[0]), w_attn_out[0].astype(BF16),
              w_ssm_out[0].astype(BF16), w_out[0].astype(BF16), final_gain.reshape(1, d)]
    return _main(x, s, consts)
```

```python
import functools
import math

import jax
import jax.numpy as jnp
from jax import lax
from jax.experimental import pallas as pl
from jax.experimental.pallas import tpu as pltpu

F32 = jnp.float32
BF16 = jnp.bfloat16

CHUNK = 64
LEFT_CHUNKS = 8
BAND = (LEFT_CHUNKS + 1) * CHUNK
HIST = LEFT_CHUNKS * CHUNK
N_HEADS = 8
HEAD_DIM = 64
ATTN_WIDTH = N_HEADS * HEAD_DIM
MAX_REL = 128
SSM_WIDTH = 512
SSM_GROUP = 16
SSM_GROUPS = SSM_WIDTH // SSM_GROUP
SSM_STATE = 64
N_STATES = SSM_GROUPS * SSM_STATE
NORM_EPS = 1e-6
NEG_INF = -1e30
LOG2E = math.log2(math.e)

VMEM_LIMIT_BYTES = 56 * 1024 * 1024


def _rms_norm(x, gain):
    inv = lax.rsqrt(jnp.mean(x * x, axis=-1, keepdims=True) + NORM_EPS)
    return x * inv * gain


def _const_spec(shape):
    zeros = (0,) * len(shape)
    return pl.BlockSpec(shape, lambda *_: zeros, pipeline_mode=pl.Buffered(1))


LANE_TILE = 128
SLAB_PITCH = 72


def _to_time_major(rows_bt, slab_s, steps, batch):
    n_slabs = rows_bt.shape[1] // LANE_TILE
    for j in range(n_slabs):
        for b in range(batch):
            slab_s[j, SLAB_PITCH * b:SLAB_PITCH * b + steps, :] = (
                rows_bt[steps * b:steps * (b + 1), LANE_TILE * j:LANE_TILE * (j + 1)])
    return jnp.concatenate(
        [jnp.concatenate([slab_s[j, pl.ds(t, batch, stride=SLAB_PITCH), :] for j in range(n_slabs)], axis=1)
         for t in range(steps)], axis=0)


def _from_time_major(rows_tb, slab_s, out_ref, steps, batch):
    n_slabs = rows_tb.shape[1] // LANE_TILE
    for t in range(steps):
        for j in range(n_slabs):
            slab_s[j, pl.ds(t, batch, stride=SLAB_PITCH), :] = (
                rows_tb[batch * t:batch * (t + 1), LANE_TILE * j:LANE_TILE * (j + 1)])
    for b in range(batch):
        out_ref[b] = jnp.concatenate(
            [slab_s[j, SLAB_PITCH * b:SLAB_PITCH * b + steps, :] for j in range(n_slabs)], axis=1)


def _ssm_kernel(x_ref, gain_ref, wu_ref, bre_ref, bim_ref, are_ref, aim_ref, cre_ref, cim_ref,
                d_ref, wglu_ref, bglu_ref, s_ref, hre_s, him_s, st_re, st_im, slab_in, slab_out,
                *, steps, batch):
    i = pl.program_id(0)

    @pl.when(i == 0)
    def _():
        st_re[...] = jnp.zeros_like(st_re)
        st_im[...] = jnp.zeros_like(st_im)

    x = x_ref[...].reshape(batch * steps, x_ref.shape[-1])
    h = _rms_norm(x, gain_ref[...]).astype(BF16)
    u_bt = jnp.dot(h, wu_ref[...], preferred_element_type=F32)
    u = _to_time_major(u_bt, slab_in, steps, batch)
    ub = u.astype(BF16)
    half_in = SSM_WIDTH // 2
    half_st = N_STATES // 2
    for hf in range(2):
        uh = ub[:, half_in * hf:half_in * (hf + 1)]
        hre_s[:, half_st * hf:half_st * (hf + 1)] = jnp.dot(uh, bre_ref[hf], preferred_element_type=F32)
        him_s[:, half_st * hf:half_st * (hf + 1)] = jnp.dot(uh, bim_ref[hf], preferred_element_type=F32)

    lanes = 512
    for q in range(N_STATES // lanes):
        sl = slice(lanes * q, lanes * (q + 1))
        ar = jnp.broadcast_to(are_ref[:, sl], (batch, lanes))
        ai = jnp.broadcast_to(aim_ref[:, sl], (batch, lanes))

        def body(t, carry, sl=sl, ar=ar, ai=ai):
            hr, hi = carry
            r0 = pl.multiple_of(t * batch, batch)
            nr = ar * hr - ai * hi + hre_s[pl.ds(r0, batch), sl]
            ni = ar * hi + ai * hr + him_s[pl.ds(r0, batch), sl]
            hre_s[pl.ds(r0, batch), sl] = nr
            him_s[pl.ds(r0, batch), sl] = ni
            return nr, ni

        hr, hi = lax.fori_loop(0, steps, body, (st_re[:, sl], st_im[:, sl]), unroll=4)
        st_re[:, sl] = hr
        st_im[:, sl] = hi

    ys = []
    for hf in range(2):
        hr = hre_s[:, half_st * hf:half_st * (hf + 1)].astype(BF16)
        hi = him_s[:, half_st * hf:half_st * (hf + 1)].astype(BF16)
        y = (jnp.dot(hr, cre_ref[hf], preferred_element_type=F32)
             - jnp.dot(hi, cim_ref[hf], preferred_element_type=F32))
        cs = slice(half_in * hf, half_in * (hf + 1))
        ys.append(y + d_ref[:, cs] * u[:, cs])
    y = jnp.concatenate(ys, axis=1)
    g = jax.nn.gelu(y)
    z = jnp.dot(g.astype(BF16), wglu_ref[...], preferred_element_type=F32) + bglu_ref[...]
    s = z[:, :SSM_WIDTH] * jax.nn.sigmoid(z[:, SSM_WIDTH:])
    _from_time_major(s, slab_out, s_ref, steps, batch)


def _ssm(x, consts, steps):
    batch, l, d = x.shape
    rows = steps * batch
    kern = functools.partial(_ssm_kernel, steps=steps, batch=batch)
    slab = pltpu.VMEM((SSM_WIDTH // LANE_TILE, SLAB_PITCH * batch, LANE_TILE), F32)
    return pl.pallas_call(
        kern,
        grid=(l // steps,),
        in_specs=[pl.BlockSpec((batch, steps, d), lambda i: (0, i, 0))]
        + [_const_spec(c.shape) for c in consts],
        out_specs=pl.BlockSpec((batch, steps, SSM_WIDTH), lambda i: (0, i, 0)),
        out_shape=jax.ShapeDtypeStruct((batch, l, SSM_WIDTH), F32),
        scratch_shapes=[
            pltpu.VMEM((rows, N_STATES), F32), pltpu.VMEM((rows, N_STATES), F32),
            pltpu.VMEM((batch, N_STATES), F32), pltpu.VMEM((batch, N_STATES), F32),
            slab, slab,
        ],
        compiler_params=pltpu.CompilerParams(
            dimension_semantics=("arbitrary",), vmem_limit_bytes=VMEM_LIMIT_BYTES),
        name="ssm",
    )(x, *consts)


PAIR = 2 * CHUNK
WINDOW = BAND + CHUNK


def _attn_unit(q_s, kt_s, v_s, bias_ref, ya_s, m, pr, first_valid):
    rows = slice(PAIR * m, PAIR * (m + 1))
    cols = slice(PAIR * pr, PAIR * (pr + 1))
    w0 = PAIR * m + first_valid
    wlen = WINDOW - first_valid
    qp = q_s[rows, cols]
    lane = lax.broadcasted_iota(jnp.int32, (PAIR, PAIR), 1)
    zero = jnp.zeros_like(qp)
    lhs = jnp.concatenate([jnp.where(lane < HEAD_DIM, qp, zero),
                           jnp.where(lane >= HEAD_DIM, qp, zero)], axis=0)
    s = jnp.dot(lhs, kt_s[cols, w0:w0 + wlen], preferred_element_type=F32)
    s = s + bias_ref[pr, :, first_valid:WINDOW]
    p = jnp.exp2(s - jnp.max(s, axis=-1, keepdims=True))
    denom = jnp.sum(p, axis=-1, keepdims=True)
    o = jnp.dot(p.astype(BF16), v_s[w0:w0 + wlen, cols], preferred_element_type=F32)
    o = o / denom
    ya_s[rows, cols] = jnp.where(lane < HEAD_DIM, o[:PAIR], o[PAIR:])


def _main_kernel(x_ref, s_ref, gain_ref, wq_ref, wkt_ref, wv_ref, wz_ref, wg_ref, gb_ref,
                 bias_ref, wao_ref, wso_ref, wo_ref, fgain_ref, o_ref, kt_s, v_s, q_s, ya_s, *, tm):
    i = pl.program_id(1)
    x = x_ref[...]
    h = _rms_norm(x, gain_ref[...]).astype(BF16)

    q_s[...] = jnp.dot(h, wq_ref[...], preferred_element_type=F32).astype(BF16)
    kt_new = lax.dot_general(wkt_ref[...], h, (((1,), (1,)), ((), ())), preferred_element_type=F32)
    v_new = jnp.dot(h, wv_ref[...], preferred_element_type=F32)

    @pl.when(i > 0)
    def _():
        kt_s[:, 0:HIST] = kt_s[:, tm:tm + HIST]
        v_s[0:HIST, :] = v_s[tm:tm + HIST, :]

    kt_s[:, HIST:HIST + tm] = kt_new.astype(BF16)
    v_s[HIST:HIST + tm, :] = v_new.astype(BF16)

    def attention(masked_start):
        for m in range(tm // PAIR):
            first_valid = max(HIST - PAIR * m, 0) if masked_start else 0
            for pr in range(N_HEADS // 2):
                _attn_unit(q_s, kt_s, v_s, bias_ref, ya_s, m, pr, first_valid)

    @pl.when(i == 0)
    def _():
        attention(True)

    @pl.when(i > 0)
    def _():
        attention(False)

    z = jnp.dot(h, wz_ref[...], preferred_element_type=F32)
    ya = ya_s[...] * jax.nn.silu(z[:, :ATTN_WIDTH])
    yao = jnp.dot(ya.astype(BF16), wao_ref[...], preferred_element_type=F32)
    ys = s_ref[...] * jax.nn.silu(z[:, ATTN_WIDTH:])
    yso = jnp.dot(ys.astype(BF16), wso_ref[...], preferred_element_type=F32)
    g = jnp.dot(h, wg_ref[...], preferred_element_type=F32) + gb_ref[...]
    d = x.shape[-1]
    merged = jax.nn.sigmoid(g[:, :d]) * yao + jax.nn.sigmoid(g[:, d:]) * yso
    xn = x + jnp.dot(merged.astype(BF16), wo_ref[...], preferred_element_type=F32)
    o_ref[...] = _rms_norm(xn, fgain_ref[...])


def _main(x, s, consts):
    b, l, d = x.shape
    tm = HIST
    kern = functools.partial(_main_kernel, tm=tm)
    return pl.pallas_call(
        kern,
        grid=(b, l // tm),
        in_specs=[
            pl.BlockSpec((None, tm, d), lambda bi, i: (bi, i, 0)),
            pl.BlockSpec((None, tm, SSM_WIDTH), lambda bi, i: (bi, i, 0)),
        ] + [_const_spec(c.shape) for c in consts],
        out_specs=pl.BlockSpec((None, tm, d), lambda bi, i: (bi, i, 0)),
        out_shape=jax.ShapeDtypeStruct((b, l, d), F32),
        scratch_shapes=[
            pltpu.VMEM((ATTN_WIDTH, HIST + tm), BF16),
            pltpu.VMEM((HIST + tm, ATTN_WIDTH), BF16),
            pltpu.VMEM((tm, ATTN_WIDTH), BF16), pltpu.VMEM((tm, ATTN_WIDTH), F32),
        ],
        compiler_params=pltpu.CompilerParams(
            dimension_semantics=("arbitrary", "arbitrary"), vmem_limit_bytes=VMEM_LIMIT_BYTES),
        name="main_block",
    )(x, s, *consts)


def _block_diag_halves(blocks):
    g, r, c = blocks.shape
    hg = g // 2
    eye = jnp.eye(hg, dtype=blocks.dtype)
    bl = blocks.reshape(2, hg, r, c)
    out = bl[:, :, :, None, :] * eye[None, :, None, :, None]
    return out.reshape(2, hg * r, hg * c)


def _ssm_params(a_re, a_im, log_dt, b_re, b_im, c_re, c_im):
    dt = jnp.exp(log_dt)[:, None]
    mag = jnp.exp(a_re * dt)
    ab_re = mag * jnp.cos(a_im * dt)
    ab_im = mag * jnp.sin(a_im * dt)
    n_re = ab_re - 1.0
    n_im = ab_im
    den = a_re * a_re + a_im * a_im
    f_re = ((n_re * a_re + n_im * a_im) / den)[..., None]
    f_im = ((n_im * a_re - n_re * a_im) / den)[..., None]
    bb_re = f_re * b_re - f_im * b_im
    bb_im = f_re * b_im + f_im * b_re
    bre = _block_diag_halves(jnp.swapaxes(bb_re, 1, 2)).astype(BF16)
    bim = _block_diag_halves(jnp.swapaxes(bb_im, 1, 2)).astype(BF16)
    cre = _block_diag_halves(jnp.swapaxes(c_re, 1, 2)).astype(BF16)
    cim = _block_diag_halves(jnp.swapaxes(c_im, 1, 2)).astype(BF16)
    return bre, bim, ab_re.reshape(1, -1), ab_im.reshape(1, -1), cre, cim


def _attn_bias_tables(rel_bias):
    n_rel = rel_bias.shape[1]
    ext = jnp.concatenate(
        [rel_bias, jnp.broadcast_to(rel_bias[:, -1:], (N_HEADS, CHUNK + BAND - 1 - n_rel))], axis=1)
    by_rev_k = jnp.stack([ext[:, q:q + BAND] for q in range(CHUNK)], axis=1)
    bias = by_rev_k[:, :, ::-1] * LOG2E
    masked = jnp.full((N_HEADS, CHUNK, CHUNK), NEG_INF, F32)
    even = jnp.concatenate([bias, masked], axis=2)
    odd = jnp.concatenate([masked, bias], axis=2)
    per_head = jnp.concatenate([even, odd], axis=1)
    return per_head.reshape(N_HEADS // 2, 4 * CHUNK, WINDOW)


def kernel(x, norm_gain, w_in, rel_bias, ssm_a_re, ssm_a_im, ssm_log_dt, ssm_b_re, ssm_b_im,
           ssm_c_re, ssm_c_im, ssm_d, w_glu, b_glu, w_attn_out, w_ssm_out, gate_bias, w_out,
           final_gain):
    assert norm_gain.shape[0] == 1, "single-layer block"
    b, l, d = x.shape
    aw, sw = ATTN_WIDTH, SSM_WIDTH
    w = w_in[0]
    wq = (w[:, :aw] * (LOG2E / math.sqrt(HEAD_DIM))).astype(BF16)
    wkt = w[:, aw:2 * aw].T.astype(BF16)
    wv = w[:, 2 * aw:3 * aw].astype(BF16)
    wz = jnp.concatenate([w[:, 3 * aw:4 * aw], w[:, 4 * aw + sw:4 * aw + 2 * sw]], axis=1).astype(BF16)
    w_u = w[:, 4 * aw:4 * aw + sw].astype(BF16)
    wg = w[:, 4 * aw + 2 * sw:].astype(BF16)
    gain = norm_gain[0].reshape(1, d)

    bre, bim, are, aim, cre, cim = _ssm_params(
        ssm_a_re[0], ssm_a_im[0], ssm_log_dt[0], ssm_b_re[0], ssm_b_im[0], ssm_c_re[0], ssm_c_im[0])
    s = _ssm(x, [gain, w_u, bre, bim, are, aim, cre, cim, ssm_d[0].reshape(1, sw),
                 w_glu[0].astype(BF16), b_glu[0].reshape(1, 2 * sw)], steps=64)

    consts = [gain, wq, wkt, wv, wz, wg, gate_bias[0].reshape(1, 2 * d),
              _attn_bias_tables(rel_bias[0]), w_attn_out[0].astype(BF16),
              w_ssm_out[0].astype(BF16), w_out[0].astype(BF16), final_gain.reshape(1, d)]
    return _main(x, s, consts)
```

```python
import functools
import math

import jax
import jax.numpy as jnp
from jax import lax
from jax.experimental import pallas as pl
from jax.experimental.pallas import tpu as pltpu

F32 = jnp.float32
BF16 = jnp.bfloat16

CHUNK = 64
LEFT_CHUNKS = 8
BAND = (LEFT_CHUNKS + 1) * CHUNK
HIST = LEFT_CHUNKS * CHUNK
N_HEADS = 8
HEAD_DIM = 64
ATTN_WIDTH = N_HEADS * HEAD_DIM
MAX_REL = 128
SSM_WIDTH = 512
SSM_GROUP = 16
SSM_GROUPS = SSM_WIDTH // SSM_GROUP
SSM_STATE = 64
NORM_EPS = 1e-6
NEG_INF = -1e30
LOG2E = math.log2(math.e)

VMEM_LIMIT_BYTES = 56 * 1024 * 1024


def _rms_norm(x, gain):
    inv = lax.rsqrt(jnp.mean(x * x, axis=-1, keepdims=True) + NORM_EPS)
    return x * inv * gain


def _const_spec(shape):
    zeros = (0,) * len(shape)
    return pl.BlockSpec(shape, lambda *_: zeros, pipeline_mode=pl.Buffered(1))


LANE_TILE = 128
SSM_CHUNK = LANE_TILE // SSM_GROUP
GROUPS_PER_TILE = LANE_TILE // SSM_GROUP
STATE_LANES = 2 * SSM_STATE


def _block_transpose(arrs):
    arrs = list(arrs)
    lane_block = lax.broadcasted_iota(jnp.int32, arrs[0].shape, 1) // SSM_GROUP
    for d in (4, 2, 1):
        low = (lane_block & d) == 0
        for lo in range(len(arrs)):
            if lo & d:
                continue
            a, b = arrs[lo], arrs[lo + d]
            arrs[lo] = jnp.where(low, a, pltpu.roll(b, SSM_GROUP * d, axis=1))
            arrs[lo + d] = jnp.where(low, pltpu.roll(a, LANE_TILE - SSM_GROUP * d, axis=1), b)
    return arrs


def _ssm_kernel(x_ref, gain_ref, wu_ref, mw_ref, p_ref, aa_ref, ab_ref, d_ref, wglu_ref, bglu_ref,
                s_ref, slab_in, slab_out, ur_s, ug_s, yi_s, sc_s, scw_s, hp_s, st, st_sw,
                *, steps, batch):
    i = pl.program_id(0)
    n_chunks = steps // SSM_CHUNK
    rows_cb = n_chunks * batch
    pitch = steps + 8
    n_tiles = SSM_WIDTH // LANE_TILE

    @pl.when(i == 0)
    def _():
        st[...] = jnp.zeros_like(st)
        st_sw[...] = jnp.zeros_like(st_sw)

    x = x_ref[...].reshape(batch * steps, x_ref.shape[-1])
    h = _rms_norm(x, gain_ref[...]).astype(BF16)
    u_bt = jnp.dot(h, wu_ref[...], preferred_element_type=F32)

    for j in range(n_tiles):
        for b in range(batch):
            slab_in[j, pitch * b:pitch * b + steps, :] = (
                u_bt[steps * b:steps * (b + 1), LANE_TILE * j:LANE_TILE * (j + 1)])
    for j in range(n_tiles):
        lanes_j = slice(LANE_TILE * j, LANE_TILE * (j + 1))
        by_step = [
            jnp.concatenate([slab_in[j, pl.ds(SSM_CHUNK * c + tau, batch, stride=pitch), :]
                             for c in range(n_chunks)], axis=0)
            for tau in range(SSM_CHUNK)]
        for tau in range(SSM_CHUNK):
            ur_s[rows_cb * tau:rows_cb * (tau + 1), lanes_j] = by_step[tau]
        by_group = _block_transpose(by_step)
        for g8 in range(GROUPS_PER_TILE):
            ug_s[GROUPS_PER_TILE * j + g8] = by_group[g8].astype(BF16)

    for g in range(SSM_GROUPS):
        r = jnp.dot(ug_s[g], mw_ref[g], preferred_element_type=F32)
        yi_s[g] = r[:, :LANE_TILE]
        sc_s[:, STATE_LANES * g:STATE_LANES * (g + 1)] = r[:, LANE_TILE:LANE_TILE + STATE_LANES]
        scw_s[:, STATE_LANES * g:STATE_LANES * (g + 1)] = r[:, LANE_TILE + STATE_LANES:]

    lanes = 1024
    for q in range(SSM_GROUPS * STATE_LANES // lanes):
        sl = slice(lanes * q, lanes * (q + 1))
        aa = jnp.broadcast_to(aa_ref[:, sl], (batch, lanes))
        ab = jnp.broadcast_to(ab_ref[:, sl], (batch, lanes))

        def body(c, carry, sl=sl, aa=aa, ab=ab):
            hh, hw = carry
            r0 = pl.multiple_of(c * batch, batch)
            hp_s[pl.ds(r0, batch), sl] = hh
            nh = aa * hh + ab * hw + sc_s[pl.ds(r0, batch), sl]
            nw = aa * hw - ab * hh + scw_s[pl.ds(r0, batch), sl]
            return nh, nw

        hh, hw = lax.fori_loop(0, n_chunks, body, (st[:, sl], st_sw[:, sl]), unroll=2)
        st[:, sl] = hh
        st_sw[:, sl] = hw

    for j in range(n_tiles):
        lanes_j = slice(LANE_TILE * j, LANE_TILE * (j + 1))
        by_group = []
        for g8 in range(GROUPS_PER_TILE):
            g = GROUPS_PER_TILE * j + g8
            hp = hp_s[:, STATE_LANES * g:STATE_LANES * (g + 1)].astype(BF16)
            by_group.append(yi_s[g] + jnp.dot(hp, p_ref[g], preferred_element_type=F32))
        by_step = _block_transpose(by_group)
        for tau in range(SSM_CHUNK):
            rows = slice(rows_cb * tau, rows_cb * (tau + 1))
            ur_s[rows, lanes_j] = by_step[tau] + d_ref[:, lanes_j] * ur_s[rows, lanes_j]

    g_act = jax.nn.gelu(ur_s[...])
    z = jnp.dot(g_act.astype(BF16), wglu_ref[...], preferred_element_type=F32) + bglu_ref[...]
    s = z[:, :SSM_WIDTH] * jax.nn.sigmoid(z[:, SSM_WIDTH:])

    for tau in range(SSM_CHUNK):
        for c in range(n_chunks):
            r0 = rows_cb * tau + batch * c
            for j in range(n_tiles):
                slab_out[j, pl.ds(SSM_CHUNK * c + tau, batch, stride=pitch), :] = (
                    s[r0:r0 + batch, LANE_TILE * j:LANE_TILE * (j + 1)])
    for b in range(batch):
        s_ref[b] = jnp.concatenate(
            [slab_out[j, pitch * b:pitch * b + steps, :] for j in range(n_tiles)], axis=1)


def _ssm(x, consts, steps):
    batch, l, d = x.shape
    n_tiles = SSM_WIDTH // LANE_TILE
    rows_cb = steps // SSM_CHUNK * batch
    kern = functools.partial(_ssm_kernel, steps=steps, batch=batch)
    slab = pltpu.VMEM((n_tiles, (steps + 8) * batch, LANE_TILE), F32)
    state_cols = SSM_GROUPS * STATE_LANES
    return pl.pallas_call(
        kern,
        grid=(l // steps,),
        in_specs=[pl.BlockSpec((batch, steps, d), lambda i: (0, i, 0))]
        + [_const_spec(c.shape) for c in consts],
        out_specs=pl.BlockSpec((batch, steps, SSM_WIDTH), lambda i: (0, i, 0)),
        out_shape=jax.ShapeDtypeStruct((batch, l, SSM_WIDTH), F32),
        scratch_shapes=[
            slab, slab,
            pltpu.VMEM((steps * batch, SSM_WIDTH), F32),
            pltpu.VMEM((SSM_GROUPS, rows_cb, LANE_TILE), BF16),
            pltpu.VMEM((SSM_GROUPS, rows_cb, LANE_TILE), F32),
            pltpu.VMEM((rows_cb, state_cols), F32), pltpu.VMEM((rows_cb, state_cols), F32),
            pltpu.VMEM((rows_cb, state_cols), F32),
            pltpu.VMEM((batch, state_cols), F32), pltpu.VMEM((batch, state_cols), F32),
        ],
        compiler_params=pltpu.CompilerParams(
            dimension_semantics=("arbitrary",), vmem_limit_bytes=VMEM_LIMIT_BYTES),
        name="ssm",
    )(x, *consts)


PAIR = 2 * CHUNK
WINDOW = BAND + CHUNK


def _attn_unit(q_s, kt_s, v_s, bias_ref, ya_s, m, pr, first_valid):
    rows = slice(PAIR * m, PAIR * (m + 1))
    cols = slice(PAIR * pr, PAIR * (pr + 1))
    w0 = PAIR * m + first_valid
    wlen = WINDOW - first_valid
    qp = q_s[rows, cols]
    lane = lax.broadcasted_iota(jnp.int32, (PAIR, PAIR), 1)
    zero = jnp.zeros_like(qp)
    lhs = jnp.concatenate([jnp.where(lane < HEAD_DIM, qp, zero),
                           jnp.where(lane >= HEAD_DIM, qp, zero)], axis=0)
    s = jnp.dot(lhs, kt_s[cols, w0:w0 + wlen], preferred_element_type=F32)
    s = s + bias_ref[pr, :, first_valid:WINDOW]
    p = jnp.exp2(s - jnp.max(s, axis=-1, keepdims=True))
    denom = jnp.sum(p, axis=-1, keepdims=True)
    o = jnp.dot(p.astype(BF16), v_s[w0:w0 + wlen, cols], preferred_element_type=F32)
    o = o / denom
    ya_s[rows, cols] = jnp.where(lane < HEAD_DIM, o[:PAIR], o[PAIR:])


def _main_kernel(x_ref, s_ref, gain_ref, wq_ref, wkt_ref, wv_ref, wz_ref, wg_ref, gb_ref,
                 bias_ref, wao_ref, wso_ref, wo_ref, fgain_ref, o_ref, kt_s, v_s, q_s, ya_s, *, tm):
    i = pl.program_id(1)
    x = x_ref[...]
    h = _rms_norm(x, gain_ref[...]).astype(BF16)

    q_s[...] = jnp.dot(h, wq_ref[...], preferred_element_type=F32).astype(BF16)
    kt_new = lax.dot_general(wkt_ref[...], h, (((1,), (1,)), ((), ())), preferred_element_type=F32)
    v_new = jnp.dot(h, wv_ref[...], preferred_element_type=F32)

    @pl.when(i > 0)
    def _():
        kt_s[:, 0:HIST] = kt_s[:, tm:tm + HIST]
        v_s[0:HIST, :] = v_s[tm:tm + HIST, :]

    kt_s[:, HIST:HIST + tm] = kt_new.astype(BF16)
    v_s[HIST:HIST + tm, :] = v_new.astype(BF16)

    def attention(masked_start):
        for m in range(tm // PAIR):
            first_valid = max(HIST - PAIR * m, 0) if masked_start else 0
            for pr in range(N_HEADS // 2):
                _attn_unit(q_s, kt_s, v_s, bias_ref, ya_s, m, pr, first_valid)

    @pl.when(i == 0)
    def _():
        attention(True)

    @pl.when(i > 0)
    def _():
        attention(False)

    z = jnp.dot(h, wz_ref[...], preferred_element_type=F32)
    ya = ya_s[...] * jax.nn.silu(z[:, :ATTN_WIDTH])
    yao = jnp.dot(ya.astype(BF16), wao_ref[...], preferred_element_type=F32)
    ys = s_ref[...] * jax.nn.silu(z[:, ATTN_WIDTH:])
    yso = jnp.dot(ys.astype(BF16), wso_ref[...], preferred_element_type=F32)
    g = jnp.dot(h, wg_ref[...], preferred_element_type=F32) + gb_ref[...]
    d = x.shape[-1]
    merged = jax.nn.sigmoid(g[:, :d]) * yao + jax.nn.sigmoid(g[:, d:]) * yso
    xn = x + jnp.dot(merged.astype(BF16), wo_ref[...], preferred_element_type=F32)
    o_ref[...] = _rms_norm(xn, fgain_ref[...])


def _main(x, s, consts):
    b, l, d = x.shape
    tm = HIST
    kern = functools.partial(_main_kernel, tm=tm)
    return pl.pallas_call(
        kern,
        grid=(b, l // tm),
        in_specs=[
            pl.BlockSpec((None, tm, d), lambda bi, i: (bi, i, 0)),
            pl.BlockSpec((None, tm, SSM_WIDTH), lambda bi, i: (bi, i, 0)),
        ] + [_const_spec(c.shape) for c in consts],
        out_specs=pl.BlockSpec((None, tm, d), lambda bi, i: (bi, i, 0)),
        out_shape=jax.ShapeDtypeStruct((b, l, d), F32),
        scratch_shapes=[
            pltpu.VMEM((ATTN_WIDTH, HIST + tm), BF16),
            pltpu.VMEM((HIST + tm, ATTN_WIDTH), BF16),
            pltpu.VMEM((tm, ATTN_WIDTH), BF16), pltpu.VMEM((tm, ATTN_WIDTH), F32),
        ],
        compiler_params=pltpu.CompilerParams(
            dimension_semantics=("arbitrary", "arbitrary"), vmem_limit_bytes=VMEM_LIMIT_BYTES),
        name="main_block",
    )(x, s, *consts)


def _ssm_chunk_params(a_re, a_im, log_dt, b_re, b_im, c_re, c_im):
    hi = lax.Precision.HIGHEST
    n = SSM_CHUNK
    dt = jnp.exp(log_dt)[:, None]
    k = jnp.arange(n + 1, dtype=F32)[:, None, None]
    mag = jnp.exp(a_re * dt * k)
    pw_re = mag * jnp.cos(a_im * dt * k)
    pw_im = mag * jnp.sin(a_im * dt * k)
    n_re = pw_re[1] - 1.0
    n_im = pw_im[1]
    den = a_re * a_re + a_im * a_im
    f_re = ((n_re * a_re + n_im * a_im) / den)[..., None]
    f_im = ((n_im * a_re - n_re * a_im) / den)[..., None]
    bb_re = f_re * b_re - f_im * b_im
    bb_im = f_re * b_im + f_im * b_re
    w_re = pw_re[:n, :, :, None] * bb_re - pw_im[:n, :, :, None] * bb_im
    w_im = pw_re[:n, :, :, None] * bb_im + pw_im[:n, :, :, None] * bb_re
    kern = (jnp.einsum("gop,kgpc->gkco", c_re, w_re, precision=hi)
            - jnp.einsum("gop,kgpc->gkco", c_im, w_im, precision=hi))
    zeros = jnp.zeros_like(kern)
    m_rows = [jnp.concatenate([zeros[:, :sig], kern[:, :n - sig]], axis=1) for sig in range(n)]
    m = jnp.stack(m_rows, axis=1)
    m = m.transpose(0, 1, 3, 2, 4).reshape(SSM_GROUPS, LANE_TILE, LANE_TILE)
    ws_re = w_re[::-1].transpose(1, 0, 3, 2).reshape(SSM_GROUPS, LANE_TILE, SSM_STATE)
    ws_im = w_im[::-1].transpose(1, 0, 3, 2).reshape(SSM_GROUPS, LANE_TILE, SSM_STATE)
    mw = jnp.concatenate([m, ws_re, ws_im, ws_im, ws_re], axis=2).astype(BF16)
    cp_re = c_re[None] * pw_re[1:, :, None, :] - c_im[None] * pw_im[1:, :, None, :]
    cp_im = c_re[None] * pw_im[1:, :, None, :] + c_im[None] * pw_re[1:, :, None, :]
    p_re = cp_re.transpose(1, 3, 0, 2).reshape(SSM_GROUPS, SSM_STATE, LANE_TILE)
    p_im = cp_im.transpose(1, 3, 0, 2).reshape(SSM_GROUPS, SSM_STATE, LANE_TILE)
    p = jnp.concatenate([p_re, -p_im], axis=1).astype(BF16)
    aa = jnp.concatenate([pw_re[n], pw_re[n]], axis=1).reshape(1, -1)
    ab = jnp.concatenate([-pw_im[n], pw_im[n]], axis=1).reshape(1, -1)
    return mw, p, aa, ab


def _attn_bias_tables(rel_bias):
    n_rel = rel_bias.shape[1]
    ext = jnp.concatenate(
        [rel_bias, jnp.broadcast_to(rel_bias[:, -1:], (N_HEADS, CHUNK + BAND - 1 - n_rel))], axis=1)
    by_rev_k = jnp.stack([ext[:, q:q + BAND] for q in range(CHUNK)], axis=1)
    bias = by_rev_k[:, :, ::-1] * LOG2E
    masked = jnp.full((N_HEADS, CHUNK, CHUNK), NEG_INF, F32)
    even = jnp.concatenate([bias, masked], axis=2)
    odd = jnp.concatenate([masked, bias], axis=2)
    per_head = jnp.concatenate([even, odd], axis=1)
    return per_head.reshape(N_HEADS // 2, 4 * CHUNK, WINDOW)


def kernel(x, norm_gain, w_in, rel_bias, ssm_a_re, ssm_a_im, ssm_log_dt, ssm_b_re, ssm_b_im,
           ssm_c_re, ssm_c_im, ssm_d, w_glu, b_glu, w_attn_out, w_ssm_out, gate_bias, w_out,
           final_gain):
    assert norm_gain.shape[0] == 1, "single-layer block"
    b, l, d = x.shape
    aw, sw = ATTN_WIDTH, SSM_WIDTH
    w = w_in[0]
    wq = (w[:, :aw] * (LOG2E / math.sqrt(HEAD_DIM))).astype(BF16)
    wkt = w[:, aw:2 * aw].T.astype(BF16)
    wv = w[:, 2 * aw:3 * aw].astype(BF16)
    wz = jnp.concatenate([w[:, 3 * aw:4 * aw], w[:, 4 * aw + sw:4 * aw + 2 * sw]], axis=1).astype(BF16)
    w_u = w[:, 4 * aw:4 * aw + sw].astype(BF16)
    wg = w[:, 4 * aw + 2 * sw:].astype(BF16)
    gain = norm_gain[0].reshape(1, d)

    mw, p, aa, ab = _ssm_chunk_params(
        ssm_a_re[0], ssm_a_im[0], ssm_log_dt[0], ssm_b_re[0], ssm_b_im[0], ssm_c_re[0], ssm_c_im[0])
    s = _ssm(x, [gain, w_u, mw, p, aa, ab, ssm_d[0].reshape(1, sw),
                 w_glu[0].astype(BF16), b_glu[0].reshape(1, 2 * sw)], steps=128)

    consts = [gain, wq, wkt, wv, wz, wg, gate_bias[0].reshape(1, 2 * d),
              _attn_bias_tables(rel_bias[0]), w_attn_out[0].astype(BF16),
              w_ssm_out[0].astype(BF16), w_out[0].astype(BF16), final_gain.reshape(1, d)]
    return _main(x, s, consts)
```

```python
import functools
import math

import jax
import jax.numpy as jnp
from jax import lax
from jax.experimental import pallas as pl
from jax.experimental.pallas import tpu as pltpu

F32 = jnp.float32
BF16 = jnp.bfloat16

CHUNK = 64
LEFT_CHUNKS = 8
BAND = (LEFT_CHUNKS + 1) * CHUNK
HIST = LEFT_CHUNKS * CHUNK
N_HEADS = 8
HEAD_DIM = 64
ATTN_WIDTH = N_HEADS * HEAD_DIM
MAX_REL = 128
SSM_WIDTH = 512
SSM_GROUP = 16
SSM_GROUPS = SSM_WIDTH // SSM_GROUP
SSM_STATE = 64
NORM_EPS = 1e-6
NEG_INF = -1e30
LOG2E = math.log2(math.e)

VMEM_LIMIT_BYTES = 56 * 1024 * 1024


def _rms_norm(x, gain):
    inv = lax.rsqrt(jnp.mean(x * x, axis=-1, keepdims=True) + NORM_EPS)
    return x * inv * gain


def _const_spec(shape):
    zeros = (0,) * len(shape)
    return pl.BlockSpec(shape, lambda *_: zeros, pipeline_mode=pl.Buffered(1))


LANE_TILE = 128
SSM_CHUNK = LANE_TILE // SSM_GROUP
GROUPS_PER_TILE = LANE_TILE // SSM_GROUP
STATE_LANES = 2 * SSM_STATE


def _block_transpose(arrs):
    arrs = list(arrs)
    lane_block = lax.broadcasted_iota(jnp.int32, arrs[0].shape, 1) // SSM_GROUP
    for d in (4, 2, 1):
        low = (lane_block & d) == 0
        for lo in range(len(arrs)):
            if lo & d:
                continue
            a, b = arrs[lo], arrs[lo + d]
            arrs[lo] = jnp.where(low, a, pltpu.roll(b, SSM_GROUP * d, axis=1))
            arrs[lo + d] = jnp.where(low, pltpu.roll(a, LANE_TILE - SSM_GROUP * d, axis=1), b)
    return arrs


def _ssm_kernel(x_ref, gain_ref, wu_ref, mw_ref, p_ref, aa_ref, ab_ref, d_ref, wglu_ref, bglu_ref,
                s_ref, slab_in, slab_out, ur_s, ug_s, yi_s, sc_s, scw_s, hp_s, st, st_sw,
                *, steps, batch):
    i = pl.program_id(0)
    n_chunks = steps // SSM_CHUNK
    rows_cb = n_chunks * batch
    pitch = steps + 8
    n_tiles = SSM_WIDTH // LANE_TILE

    @pl.when(i == 0)
    def _():
        st[...] = jnp.zeros_like(st)
        st_sw[...] = jnp.zeros_like(st_sw)

    x = x_ref[...].reshape(batch * steps, x_ref.shape[-1])
    h = _rms_norm(x, gain_ref[...]).astype(BF16)
    u_bt = jnp.dot(h, wu_ref[...], preferred_element_type=F32)

    for j in range(n_tiles):
        for b in range(batch):
            slab_in[j, pitch * b:pitch * b + steps, :] = (
                u_bt[steps * b:steps * (b + 1), LANE_TILE * j:LANE_TILE * (j + 1)])
    for j in range(n_tiles):
        lanes_j = slice(LANE_TILE * j, LANE_TILE * (j + 1))
        by_step = [
            jnp.concatenate([slab_in[j, pl.ds(SSM_CHUNK * c + tau, batch, stride=pitch), :]
                             for c in range(n_chunks)], axis=0)
            for tau in range(SSM_CHUNK)]
        for tau in range(SSM_CHUNK):
            ur_s[rows_cb * tau:rows_cb * (tau + 1), lanes_j] = by_step[tau]
        by_group = _block_transpose(by_step)
        for g8 in range(GROUPS_PER_TILE):
            ug_s[GROUPS_PER_TILE * j + g8] = by_group[g8].astype(BF16)

    for g in range(SSM_GROUPS):
        r = jnp.dot(ug_s[g], mw_ref[g], preferred_element_type=F32)
        yi_s[g] = r[:, :LANE_TILE]
        sc_s[:, STATE_LANES * g:STATE_LANES * (g + 1)] = r[:, LANE_TILE:LANE_TILE + STATE_LANES]
        scw_s[:, STATE_LANES * g:STATE_LANES * (g + 1)] = r[:, LANE_TILE + STATE_LANES:]

    lanes = 1024
    for q in range(SSM_GROUPS * STATE_LANES // lanes):
        sl = slice(lanes * q, lanes * (q + 1))
        aa = jnp.broadcast_to(aa_ref[:, sl], (batch, lanes))
        ab = jnp.broadcast_to(ab_ref[:, sl], (batch, lanes))

        def body(c, carry, sl=sl, aa=aa, ab=ab):
            hh, hw = carry
            r0 = pl.multiple_of(c * batch, batch)
            hp_s[pl.ds(r0, batch), sl] = hh
            nh = aa * hh + ab * hw + sc_s[pl.ds(r0, batch), sl]
            nw = aa * hw - ab * hh + scw_s[pl.ds(r0, batch), sl]
            return nh, nw

        hh, hw = lax.fori_loop(0, n_chunks, body, (st[:, sl], st_sw[:, sl]), unroll=2)
        st[:, sl] = hh
        st_sw[:, sl] = hw

    for j in range(n_tiles):
        lanes_j = slice(LANE_TILE * j, LANE_TILE * (j + 1))
        by_group = []
        for g8 in range(GROUPS_PER_TILE):
            g = GROUPS_PER_TILE * j + g8
            hp = hp_s[:, STATE_LANES * g:STATE_LANES * (g + 1)].astype(BF16)
            by_group.append(yi_s[g] + jnp.dot(hp, p_ref[g], preferred_element_type=F32))
        by_step = _block_transpose(by_group)
        for tau in range(SSM_CHUNK):
            rows = slice(rows_cb * tau, rows_cb * (tau + 1))
            ur_s[rows, lanes_j] = by_step[tau] + d_ref[:, lanes_j] * ur_s[rows, lanes_j]

    g_act = jax.nn.gelu(ur_s[...])
    z = jnp.dot(g_act.astype(BF16), wglu_ref[...], preferred_element_type=F32) + bglu_ref[...]
    s = z[:, :SSM_WIDTH] * jax.nn.sigmoid(z[:, SSM_WIDTH:])

    for tau in range(SSM_CHUNK):
        for c in range(n_chunks):
            r0 = rows_cb * tau + batch * c
            for j in range(n_tiles):
                slab_out[j, pl.ds(SSM_CHUNK * c + tau, batch, stride=pitch), :] = (
                    s[r0:r0 + batch, LANE_TILE * j:LANE_TILE * (j + 1)])
    for b in range(batch):
        s_ref[b] = jnp.concatenate(
            [slab_out[j, pitch * b:pitch * b + steps, :] for j in range(n_tiles)], axis=1)


def _ssm(x, consts, steps):
    batch, l, d = x.shape
    n_tiles = SSM_WIDTH // LANE_TILE
    rows_cb = steps // SSM_CHUNK * batch
    kern = functools.partial(_ssm_kernel, steps=steps, batch=batch)
    slab = pltpu.VMEM((n_tiles, (steps + 8) * batch, LANE_TILE), F32)
    state_cols = SSM_GROUPS * STATE_LANES
    return pl.pallas_call(
        kern,
        grid=(l // steps,),
        in_specs=[pl.BlockSpec((batch, steps, d), lambda i: (0, i, 0))]
        + [_const_spec(c.shape) for c in consts],
        out_specs=pl.BlockSpec((batch, steps, SSM_WIDTH), lambda i: (0, i, 0)),
        out_shape=jax.ShapeDtypeStruct((batch, l, SSM_WIDTH), F32),
        scratch_shapes=[
            slab, slab,
            pltpu.VMEM((steps * batch, SSM_WIDTH), F32),
            pltpu.VMEM((SSM_GROUPS, rows_cb, LANE_TILE), BF16),
            pltpu.VMEM((SSM_GROUPS, rows_cb, LANE_TILE), F32),
            pltpu.VMEM((rows_cb, state_cols), F32), pltpu.VMEM((rows_cb, state_cols), F32),
            pltpu.VMEM((rows_cb, state_cols), F32),
            pltpu.VMEM((batch, state_cols), F32), pltpu.VMEM((batch, state_cols), F32),
        ],
        compiler_params=pltpu.CompilerParams(
            dimension_semantics=("arbitrary",), vmem_limit_bytes=VMEM_LIMIT_BYTES),
        name="ssm",
    )(x, *consts)


PAIR = 2 * CHUNK
WINDOW = BAND + CHUNK
N_PAIRS = N_HEADS // 2


class _AttnRefs:
    def __init__(self, bias_ref, qm_s, kw_s, v_s, ya_s, s_bufs, p_bufs, r_bufs):
        self.bias_ref, self.qm_s, self.kw_s, self.v_s, self.ya_s = bias_ref, qm_s, kw_s, v_s, ya_s
        self.s_bufs, self.p_bufs, self.r_bufs = s_bufs, p_bufs, r_bufs


def _unit_index(k):
    if isinstance(k, int):
        return k // N_PAIRS, k % N_PAIRS
    return lax.shift_right_logical(k, 2), lax.bitwise_and(k, N_PAIRS - 1)


def _row_start(idx, size):
    return idx * size if isinstance(idx, int) else pl.multiple_of(idx * size, size)


def _scores_stage(r, k, slot, masked_start):
    m, pr = _unit_index(k)
    kt = r.kw_s[m, pl.ds(_row_start(pr, PAIR), PAIR), :]
    s = jnp.dot(r.qm_s[pr, m], kt, preferred_element_type=F32) + r.bias_ref[pr]
    if masked_start:
        col = lax.broadcasted_iota(jnp.int32, s.shape, 1)
        s = jnp.where(col >= HIST - PAIR * m, s, NEG_INF)
    r.s_bufs[slot][...] = s


def _softmax_stage(r, slot):
    s = r.s_bufs[slot][...]
    p = jnp.exp2(s - jnp.max(s, axis=-1, keepdims=True))
    r.r_bufs[slot][...] = 1.0 / jnp.sum(p, axis=-1, keepdims=True)
    r.p_bufs[slot][...] = p.astype(BF16)


def _values_stage(r, k, slot):
    m, pr = _unit_index(k)
    vw = r.v_s[pr, pl.ds(_row_start(m, PAIR), WINDOW), :]
    o = jnp.dot(r.p_bufs[slot][...], vw, preferred_element_type=F32) * r.r_bufs[slot][...]
    lane = lax.broadcasted_iota(jnp.int32, (PAIR, PAIR), 1)
    r.ya_s[pr, pl.ds(_row_start(m, PAIR), PAIR), :] = jnp.where(lane < HEAD_DIM, o[:PAIR], o[PAIR:])


def _attention(r, n_units, masked_start):
    _scores_stage(r, 0, 0, masked_start)
    _scores_stage(r, 1, 1, masked_start)
    _softmax_stage(r, 0)

    def body(kk, carry):
        k = 2 * kk + 2
        _scores_stage(r, k, 0, masked_start)
        _softmax_stage(r, 1)
        _values_stage(r, k - 2, 0)
        _scores_stage(r, k + 1, 1, masked_start)
        _softmax_stage(r, 0)
        _values_stage(r, k - 1, 1)
        return carry

    lax.fori_loop(0, (n_units - 2) // 2, body, 0)
    _softmax_stage(r, 1)
    _values_stage(r, n_units - 2, 0)
    _values_stage(r, n_units - 1, 1)


def _main_kernel(x_ref, s_ref, gain_ref, wq_ref, wkt_ref, wv_ref, wz_ref, wg_ref, gb_ref,
                 bias_ref, wao_ref, wso_ref, wo_ref, fgain_ref, o_ref,
                 kt_s, kw_s, v_s, qm_s, ya_s, s_buf0, s_buf1, p_buf0, p_buf1, r_buf0, r_buf1, *, tm):
    i = pl.program_id(1)
    n_m = tm // PAIR
    x = x_ref[...]
    h = _rms_norm(x, gain_ref[...]).astype(BF16)

    q = jnp.dot(h, wq_ref[...], preferred_element_type=F32)
    lane = lax.broadcasted_iota(jnp.int32, (tm, PAIR), 1)
    for pr in range(N_PAIRS):
        qp = q[:, PAIR * pr:PAIR * (pr + 1)]
        lo = jnp.where(lane < HEAD_DIM, qp, 0.0).astype(BF16)
        hi = jnp.where(lane >= HEAD_DIM, qp, 0.0).astype(BF16)
        for m in range(n_m):
            qm_s[pr, m, 0:PAIR, :] = lo[PAIR * m:PAIR * (m + 1)]
            qm_s[pr, m, PAIR:2 * PAIR, :] = hi[PAIR * m:PAIR * (m + 1)]

    kt_new = lax.dot_general(wkt_ref[...], h, (((1,), (1,)), ((), ())), preferred_element_type=F32)
    v_new = jnp.dot(h, wv_ref[...], preferred_element_type=F32)

    @pl.when(i == 0)
    def _():
        kt_s[:, 0:HIST] = jnp.zeros((ATTN_WIDTH, HIST), BF16)
        v_s[:, 0:HIST, :] = jnp.zeros((N_PAIRS, HIST, PAIR), BF16)

    @pl.when(i > 0)
    def _():
        kt_s[:, 0:HIST] = kt_s[:, tm:tm + HIST]
        v_s[:, 0:HIST, :] = v_s[:, tm:tm + HIST, :]

    kt_s[:, HIST:HIST + tm] = kt_new.astype(BF16)
    for pr in range(N_PAIRS):
        v_s[pr, HIST:HIST + tm, :] = v_new[:, PAIR * pr:PAIR * (pr + 1)].astype(BF16)
    for m in range(n_m):
        kw_s[m] = kt_s[:, PAIR * m:PAIR * m + WINDOW]

    refs = _AttnRefs(bias_ref, qm_s, kw_s, v_s, ya_s,
                     (s_buf0, s_buf1), (p_buf0, p_buf1), (r_buf0, r_buf1))

    @pl.when(i == 0)
    def _():
        _attention(refs, n_m * N_PAIRS, True)

    @pl.when(i > 0)
    def _():
        _attention(refs, n_m * N_PAIRS, False)

    z = jnp.dot(h, wz_ref[...], preferred_element_type=F32)
    y_attn = jnp.concatenate([ya_s[pr] for pr in range(N_PAIRS)], axis=1)
    ya = y_attn * jax.nn.silu(z[:, :ATTN_WIDTH])
    yao = jnp.dot(ya.astype(BF16), wao_ref[...], preferred_element_type=F32)
    ys = s_ref[...] * jax.nn.silu(z[:, ATTN_WIDTH:])
    yso = jnp.dot(ys.astype(BF16), wso_ref[...], preferred_element_type=F32)
    g = jnp.dot(h, wg_ref[...], preferred_element_type=F32) + gb_ref[...]
    d = x.shape[-1]
    merged = jax.nn.sigmoid(g[:, :d]) * yao + jax.nn.sigmoid(g[:, d:]) * yso
    xn = x + jnp.dot(merged.astype(BF16), wo_ref[...], preferred_element_type=F32)
    o_ref[...] = _rms_norm(xn, fgain_ref[...])


def _main(x, s, consts):
    b, l, d = x.shape
    tm = HIST
    n_m = tm // PAIR
    kern = functools.partial(_main_kernel, tm=tm)
    unit_rows = 2 * PAIR
    return pl.pallas_call(
        kern,
        grid=(b, l // tm),
        in_specs=[
            pl.BlockSpec((None, tm, d), lambda bi, i: (bi, i, 0)),
            pl.BlockSpec((None, tm, SSM_WIDTH), lambda bi, i: (bi, i, 0)),
        ] + [_const_spec(c.shape) for c in consts],
        out_specs=pl.BlockSpec((None, tm, d), lambda bi, i: (bi, i, 0)),
        out_shape=jax.ShapeDtypeStruct((b, l, d), F32),
        scratch_shapes=[
            pltpu.VMEM((ATTN_WIDTH, HIST + tm), BF16),
            pltpu.VMEM((n_m, ATTN_WIDTH, WINDOW), BF16),
            pltpu.VMEM((N_PAIRS, HIST + tm, PAIR), BF16),
            pltpu.VMEM((N_PAIRS, n_m, unit_rows, PAIR), BF16),
            pltpu.VMEM((N_PAIRS, tm, PAIR), F32),
            pltpu.VMEM((unit_rows, WINDOW), F32), pltpu.VMEM((unit_rows, WINDOW), F32),
            pltpu.VMEM((unit_rows, WINDOW), BF16), pltpu.VMEM((unit_rows, WINDOW), BF16),
            pltpu.VMEM((unit_rows, 1), F32), pltpu.VMEM((unit_rows, 1), F32),
        ],
        compiler_params=pltpu.CompilerParams(
            dimension_semantics=("arbitrary", "arbitrary"), vmem_limit_bytes=VMEM_LIMIT_BYTES),
        name="main_block",
    )(x, s, *consts)


def _ssm_chunk_params(a_re, a_im, log_dt, b_re, b_im, c_re, c_im):
    hi = lax.Precision.HIGHEST
    n = SSM_CHUNK
    dt = jnp.exp(log_dt)[:, None]
    k = jnp.arange(n + 1, dtype=F32)[:, None, None]
    mag = jnp.exp(a_re * dt * k)
    pw_re = mag * jnp.cos(a_im * dt * k)
    pw_im = mag * jnp.sin(a_im * dt * k)
    n_re = pw_re[1] - 1.0
    n_im = pw_im[1]
    den = a_re * a_re + a_im * a_im
    f_re = ((n_re * a_re + n_im * a_im) / den)[..., None]
    f_im = ((n_im * a_re - n_re * a_im) / den)[..., None]
    bb_re = f_re * b_re - f_im * b_im
    bb_im = f_re * b_im + f_im * b_re
    w_re = pw_re[:n, :, :, None] * bb_re - pw_im[:n, :, :, None] * bb_im
    w_im = pw_re[:n, :, :, None] * bb_im + pw_im[:n, :, :, None] * bb_re
    kern = (jnp.einsum("gop,kgpc->gkco", c_re, w_re, precision=hi)
            - jnp.einsum("gop,kgpc->gkco", c_im, w_im, precision=hi))
    zeros = jnp.zeros_like(kern)
    m_rows = [jnp.concatenate([zeros[:, :sig], kern[:, :n - sig]], axis=1) for sig in range(n)]
    m = jnp.stack(m_rows, axis=1)
    m = m.transpose(0, 1, 3, 2, 4).reshape(SSM_GROUPS, LANE_TILE, LANE_TILE)
    ws_re = w_re[::-1].transpose(1, 0, 3, 2).reshape(SSM_GROUPS, LANE_TILE, SSM_STATE)
    ws_im = w_im[::-1].transpose(1, 0, 3, 2).reshape(SSM_GROUPS, LANE_TILE, SSM_STATE)
    mw = jnp.concatenate([m, ws_re, ws_im, ws_im, ws_re], axis=2).astype(BF16)
    cp_re = c_re[None] * pw_re[1:, :, None, :] - c_im[None] * pw_im[1:, :, None, :]
    cp_im = c_re[None] * pw_im[1:, :, None, :] + c_im[None] * pw_re[1:, :, None, :]
    p_re = cp_re.transpose(1, 3, 0, 2).reshape(SSM_GROUPS, SSM_STATE, LANE_TILE)
    p_im = cp_im.transpose(1, 3, 0, 2).reshape(SSM_GROUPS, SSM_STATE, LANE_TILE)
    p = jnp.concatenate([p_re, -p_im], axis=1).astype(BF16)
    aa = jnp.concatenate([pw_re[n], pw_re[n]], axis=1).reshape(1, -1)
    ab = jnp.concatenate([-pw_im[n], pw_im[n]], axis=1).reshape(1, -1)
    return mw, p, aa, ab


def _attn_bias_tables(rel_bias):
    n_rel = rel_bias.shape[1]
    ext = jnp.concatenate(
        [rel_bias, jnp.broadcast_to(rel_bias[:, -1:], (N_HEADS, CHUNK + BAND - 1 - n_rel))], axis=1)
    by_rev_k = jnp.stack([ext[:, q:q + BAND] for q in range(CHUNK)], axis=1)
    bias = by_rev_k[:, :, ::-1] * LOG2E
    masked = jnp.full((N_HEADS, CHUNK, CHUNK), NEG_INF, F32)
    even = jnp.concatenate([bias, masked], axis=2)
    odd = jnp.concatenate([masked, bias], axis=2)
    per_head = jnp.concatenate([even, odd], axis=1)
    return per_head.reshape(N_HEADS // 2, 4 * CHUNK, WINDOW)


def kernel(x, norm_gain, w_in, rel_bias, ssm_a_re, ssm_a_im, ssm_log_dt, ssm_b_re, ssm_b_im,
           ssm_c_re, ssm_c_im, ssm_d, w_glu, b_glu, w_attn_out, w_ssm_out, gate_bias, w_out,
           final_gain):
    assert norm_gain.shape[0] == 1, "single-layer block"
    b, l, d = x.shape
    aw, sw = ATTN_WIDTH, SSM_WIDTH
    w = w_in[0]
    wq = (w[:, :aw] * (LOG2E / math.sqrt(HEAD_DIM))).astype(BF16)
    wkt = w[:, aw:2 * aw].T.astype(BF16)
    wv = w[:, 2 * aw:3 * aw].astype(BF16)
    wz = jnp.concatenate([w[:, 3 * aw:4 * aw], w[:, 4 * aw + sw:4 * aw + 2 * sw]], axis=1).astype(BF16)
    w_u = w[:, 4 * aw:4 * aw + sw].astype(BF16)
    wg = w[:, 4 * aw + 2 * sw:].astype(BF16)
    gain = norm_gain[0].reshape(1, d)

    mw, p, aa, ab = _ssm_chunk_params(
        ssm_a_re[0], ssm_a_im[0], ssm_log_dt[0], ssm_b_re[0], ssm_b_im[0], ssm_c_re[0], ssm_c_im[0])
    s = _ssm(x, [gain, w_u, mw, p, aa, ab, ssm_d[0].reshape(1, sw),
                 w_glu[0].astype(BF16), b_glu[0].reshape(1, 2 * sw)], steps=128)

    consts = [gain, wq, wkt, wv, wz, wg, gate_bias[0].reshape(1, 2 * d),
              _attn_bias_tables(rel_bias[0]), w_attn_out[0].astype(BF16),
              w_ssm_out[0].astype(BF16), w_out[0].astype(BF16), final_gain.reshape(1, d)]
    return _main(x, s, consts)
```

```python
import functools
import math

import jax
import jax.numpy as jnp
from jax import lax
from jax.experimental import pallas as pl
from jax.experimental.pallas import tpu as pltpu

F32 = jnp.float32
BF16 = jnp.bfloat16

CHUNK = 64
LEFT_CHUNKS = 8
BAND = (LEFT_CHUNKS + 1) * CHUNK
HIST = LEFT_CHUNKS * CHUNK
N_HEADS = 8
HEAD_DIM = 64
ATTN_WIDTH = N_HEADS * HEAD_DIM
MAX_REL = 128
SSM_WIDTH = 512
SSM_GROUP = 16
SSM_GROUPS = SSM_WIDTH // SSM_GROUP
SSM_STATE = 64
NORM_EPS = 1e-6
NEG_INF = -1e30
LOG2E = math.log2(math.e)

VMEM_LIMIT_BYTES = 56 * 1024 * 1024


def _rms_norm(x, gain):
    inv = lax.rsqrt(jnp.mean(x * x, axis=-1, keepdims=True) + NORM_EPS)
    return x * inv * gain


def _const_spec(shape):
    zeros = (0,) * len(shape)
    return pl.BlockSpec(shape, lambda *_: zeros, pipeline_mode=pl.Buffered(1))


LANE_TILE = 128
SSM_CHUNK = LANE_TILE // SSM_GROUP
GROUPS_PER_TILE = LANE_TILE // SSM_GROUP
STATE_LANES = 2 * SSM_STATE


def _block_transpose(arrs):
    arrs = list(arrs)
    lane_block = lax.broadcasted_iota(jnp.int32, arrs[0].shape, 1) // SSM_GROUP
    for d in (4, 2, 1):
        low = (lane_block & d) == 0
        for lo in range(len(arrs)):
            if lo & d:
                continue
            a, b = arrs[lo], arrs[lo + d]
            arrs[lo] = jnp.where(low, a, pltpu.roll(b, SSM_GROUP * d, axis=1))
            arrs[lo + d] = jnp.where(low, pltpu.roll(a, LANE_TILE - SSM_GROUP * d, axis=1), b)
    return arrs


def _ssm_kernel(x_ref, gain_ref, wu_ref, mw_ref, p_ref, aa_ref, ab_ref, d_ref, wglu_ref, bglu_ref,
                s_ref, slab_in, slab_out, ur_s, sc_s, scw_s, hp_s, st, st_sw,
                *, steps, batch):
    i = pl.program_id(0)
    n_chunks = steps // SSM_CHUNK
    rows_cb = n_chunks * batch
    pitch = steps + 8
    n_tiles = SSM_WIDTH // LANE_TILE

    @pl.when(i == 0)
    def _():
        st[...] = jnp.zeros_like(st)
        st_sw[...] = jnp.zeros_like(st_sw)

    x = x_ref[...].reshape(batch * steps, x_ref.shape[-1])
    h = _rms_norm(x, gain_ref[...]).astype(BF16)
    u_bt = jnp.dot(h, wu_ref[...], preferred_element_type=F32)

    for j in range(n_tiles):
        for b in range(batch):
            slab_in[j, pitch * b:pitch * b + steps, :] = (
                u_bt[steps * b:steps * (b + 1), LANE_TILE * j:LANE_TILE * (j + 1)])
    group_lanes = GROUPS_PER_TILE * STATE_LANES
    for j in range(n_tiles):
        lanes_j = slice(LANE_TILE * j, LANE_TILE * (j + 1))
        by_step = [
            jnp.concatenate([slab_in[j, pl.ds(SSM_CHUNK * c + tau, batch, stride=pitch), :]
                             for c in range(n_chunks)], axis=0)
            for tau in range(SSM_CHUNK)]
        for tau in range(SSM_CHUNK):
            ur_s[rows_cb * tau:rows_cb * (tau + 1), lanes_j] = by_step[tau]
        by_group = _block_transpose(by_step)

        intra = []
        for g8 in range(GROUPS_PER_TILE):
            g = GROUPS_PER_TILE * j + g8
            r = jnp.dot(by_group[g8].astype(BF16), mw_ref[g], preferred_element_type=F32)
            intra.append(r[:, :LANE_TILE])
            sc_s[:, STATE_LANES * g:STATE_LANES * (g + 1)] = r[:, LANE_TILE:LANE_TILE + STATE_LANES]
            scw_s[:, STATE_LANES * g:STATE_LANES * (g + 1)] = r[:, LANE_TILE + STATE_LANES:]

        sl = slice(group_lanes * j, group_lanes * (j + 1))
        aa = jnp.broadcast_to(aa_ref[:, sl], (batch, group_lanes))
        ab = jnp.broadcast_to(ab_ref[:, sl], (batch, group_lanes))
        hh, hw = st[:, sl], st_sw[:, sl]
        for c in range(n_chunks):
            rows = slice(batch * c, batch * (c + 1))
            hp_s[rows, sl] = hh
            hh, hw = (aa * hh + ab * hw + sc_s[rows, sl], aa * hw - ab * hh + scw_s[rows, sl])
        st[:, sl] = hh
        st_sw[:, sl] = hw

        by_group = []
        for g8 in range(GROUPS_PER_TILE):
            g = GROUPS_PER_TILE * j + g8
            hp = hp_s[:, STATE_LANES * g:STATE_LANES * (g + 1)].astype(BF16)
            by_group.append(intra[g8] + jnp.dot(hp, p_ref[g], preferred_element_type=F32))
        by_step = _block_transpose(by_group)
        for tau in range(SSM_CHUNK):
            rows = slice(rows_cb * tau, rows_cb * (tau + 1))
            ur_s[rows, lanes_j] = by_step[tau] + d_ref[:, lanes_j] * ur_s[rows, lanes_j]

    g_act = jax.nn.gelu(ur_s[...])
    z = jnp.dot(g_act.astype(BF16), wglu_ref[...], preferred_element_type=F32) + bglu_ref[...]
    s = z[:, :SSM_WIDTH] * jax.nn.sigmoid(z[:, SSM_WIDTH:])

    for tau in range(SSM_CHUNK):
        for c in range(n_chunks):
            r0 = rows_cb * tau + batch * c
            for j in range(n_tiles):
                slab_out[j, pl.ds(SSM_CHUNK * c + tau, batch, stride=pitch), :] = (
                    s[r0:r0 + batch, LANE_TILE * j:LANE_TILE * (j + 1)])
    for b in range(batch):
        s_ref[b] = jnp.concatenate(
            [slab_out[j, pitch * b:pitch * b + steps, :] for j in range(n_tiles)], axis=1)


def _ssm(x, consts, steps):
    batch, l, d = x.shape
    n_tiles = SSM_WIDTH // LANE_TILE
    rows_cb = steps // SSM_CHUNK * batch
    kern = functools.partial(_ssm_kernel, steps=steps, batch=batch)
    slab = pltpu.VMEM((n_tiles, (steps + 8) * batch, LANE_TILE), F32)
    state_cols = SSM_GROUPS * STATE_LANES
    return pl.pallas_call(
        kern,
        grid=(l // steps,),
        in_specs=[pl.BlockSpec((batch, steps, d), lambda i: (0, i, 0))]
        + [_const_spec(c.shape) for c in consts],
        out_specs=pl.BlockSpec((batch, steps, SSM_WIDTH), lambda i: (0, i, 0)),
        out_shape=jax.ShapeDtypeStruct((batch, l, SSM_WIDTH), F32),
        scratch_shapes=[
            slab, slab,
            pltpu.VMEM((steps * batch, SSM_WIDTH), F32),
            pltpu.VMEM((rows_cb, state_cols), F32), pltpu.VMEM((rows_cb, state_cols), F32),
            pltpu.VMEM((rows_cb, state_cols), F32),
            pltpu.VMEM((batch, state_cols), F32), pltpu.VMEM((batch, state_cols), F32),
        ],
        compiler_params=pltpu.CompilerParams(
            dimension_semantics=("arbitrary",), vmem_limit_bytes=VMEM_LIMIT_BYTES),
        name="ssm",
    )(x, *consts)


PAIR = 2 * CHUNK
WINDOW = BAND + CHUNK
N_PAIRS = N_HEADS // 2


class _AttnRefs:
    def __init__(self, bias_ref, qm_s, k_s, vw_s, yat_s, s_bufs, p_bufs, r_bufs):
        self.bias_ref, self.qm_s, self.k_s, self.vw_s, self.yat_s = bias_ref, qm_s, k_s, vw_s, yat_s
        self.s_bufs, self.p_bufs, self.r_bufs = s_bufs, p_bufs, r_bufs


def _unit_index(k):
    if isinstance(k, int):
        return k // N_PAIRS, k % N_PAIRS
    return lax.shift_right_logical(k, 2), lax.bitwise_and(k, N_PAIRS - 1)


def _row_start(idx, size):
    return idx * size if isinstance(idx, int) else pl.multiple_of(idx * size, size)


def _nt_dot(a, b):
    return lax.dot_general(a, b, (((1,), (1,)), ((), ())), preferred_element_type=F32)


def _scores_stage(r, k, slot, masked_start):
    m, pr = _unit_index(k)
    kwin = r.k_s[pr, pl.ds(_row_start(m, PAIR), WINDOW), :]
    s = _nt_dot(kwin, r.qm_s[pr, m]) + r.bias_ref[pr]
    if masked_start:
        row = lax.broadcasted_iota(jnp.int32, s.shape, 0)
        s = jnp.where(row >= HIST - PAIR * m, s, NEG_INF)
    r.s_bufs[slot][...] = s


def _softmax_stage(r, slot):
    s = r.s_bufs[slot][...]
    p = jnp.exp2(s - jnp.max(s, axis=0, keepdims=True))
    r.r_bufs[slot][...] = 1.0 / jnp.sum(p, axis=0, keepdims=True)
    r.p_bufs[slot][...] = p.astype(BF16)


def _values_stage(r, k, slot):
    m, pr = _unit_index(k)
    vwin = r.vw_s[m, pl.ds(_row_start(pr, PAIR), PAIR), :]
    o = jnp.dot(vwin, r.p_bufs[slot][...], preferred_element_type=F32) * r.r_bufs[slot][...]
    r.yat_s[m, pl.ds(_row_start(pr, PAIR), PAIR), :] = jnp.concatenate(
        [o[:HEAD_DIM, :PAIR], o[HEAD_DIM:, PAIR:]], axis=0)


def _attention(r, n_units, masked_start):
    _scores_stage(r, 0, 0, masked_start)
    _scores_stage(r, 1, 1, masked_start)
    _softmax_stage(r, 0)

    def body(kk, carry):
        k = 2 * kk + 2
        _scores_stage(r, k, 0, masked_start)
        _softmax_stage(r, 1)
        _values_stage(r, k - 2, 0)
        _scores_stage(r, k + 1, 1, masked_start)
        _softmax_stage(r, 0)
        _values_stage(r, k - 1, 1)
        return carry

    lax.fori_loop(0, (n_units - 2) // 2, body, 0)
    _softmax_stage(r, 1)
    _values_stage(r, n_units - 2, 0)
    _values_stage(r, n_units - 1, 1)


def _main_kernel(x_ref, s_ref, gain_ref, wq_ref, wk_ref, wvt_ref, wzat_ref, wzs_ref, wg_ref, gb_ref,
                 bias_ref, wao_ref, wso_ref, wo_ref, fgain_ref, o_ref,
                 k_s, vt_s, vw_s, qm_s, yat_s, s_buf0, s_buf1, p_buf0, p_buf1, r_buf0, r_buf1, *, tm):
    i = pl.program_id(1)
    n_m = tm // PAIR
    x = x_ref[...]
    h = _rms_norm(x, gain_ref[...]).astype(BF16)

    q = jnp.dot(h, wq_ref[...], preferred_element_type=F32)
    lane = lax.broadcasted_iota(jnp.int32, (tm, PAIR), 1)
    for pr in range(N_PAIRS):
        qp = q[:, PAIR * pr:PAIR * (pr + 1)]
        lo = jnp.where(lane < HEAD_DIM, qp, 0.0).astype(BF16)
        hi = jnp.where(lane >= HEAD_DIM, qp, 0.0).astype(BF16)
        for m in range(n_m):
            qm_s[pr, m, 0:PAIR, :] = lo[PAIR * m:PAIR * (m + 1)]
            qm_s[pr, m, PAIR:2 * PAIR, :] = hi[PAIR * m:PAIR * (m + 1)]

    k_new = jnp.dot(h, wk_ref[...], preferred_element_type=F32)
    vt_new = _nt_dot(wvt_ref[...], h)

    @pl.when(i == 0)
    def _():
        k_s[:, 0:HIST, :] = jnp.zeros((N_PAIRS, HIST, PAIR), BF16)
        vt_s[:, 0:HIST] = jnp.zeros((ATTN_WIDTH, HIST), BF16)

    @pl.when(i > 0)
    def _():
        k_s[:, 0:HIST, :] = k_s[:, tm:tm + HIST, :]
        vt_s[:, 0:HIST] = vt_s[:, tm:tm + HIST]

    vt_s[:, HIST:HIST + tm] = vt_new.astype(BF16)
    for pr in range(N_PAIRS):
        k_s[pr, HIST:HIST + tm, :] = k_new[:, PAIR * pr:PAIR * (pr + 1)].astype(BF16)
    for m in range(n_m):
        vw_s[m] = vt_s[:, PAIR * m:PAIR * m + WINDOW]

    refs = _AttnRefs(bias_ref, qm_s, k_s, vw_s, yat_s,
                     (s_buf0, s_buf1), (p_buf0, p_buf1), (r_buf0, r_buf1))

    @pl.when(i == 0)
    def _():
        _attention(refs, n_m * N_PAIRS, True)

    @pl.when(i > 0)
    def _():
        _attention(refs, n_m * N_PAIRS, False)

    zat = _nt_dot(wzat_ref[...], h)
    yat = jnp.concatenate([yat_s[m] for m in range(n_m)], axis=1)
    ya = (yat * jax.nn.silu(zat)).T
    yao = jnp.dot(ya.astype(BF16), wao_ref[...], preferred_element_type=F32)
    zs = jnp.dot(h, wzs_ref[...], preferred_element_type=F32)
    ys = s_ref[...] * jax.nn.silu(zs)
    yso = jnp.dot(ys.astype(BF16), wso_ref[...], preferred_element_type=F32)
    g = jnp.dot(h, wg_ref[...], preferred_element_type=F32) + gb_ref[...]
    d = x.shape[-1]
    merged = jax.nn.sigmoid(g[:, :d]) * yao + jax.nn.sigmoid(g[:, d:]) * yso
    xn = x + jnp.dot(merged.astype(BF16), wo_ref[...], preferred_element_type=F32)
    o_ref[...] = _rms_norm(xn, fgain_ref[...])


def _main(x, s, consts):
    b, l, d = x.shape
    tm = HIST
    n_m = tm // PAIR
    kern = functools.partial(_main_kernel, tm=tm)
    unit_rows = 2 * PAIR
    return pl.pallas_call(
        kern,
        grid=(b, l // tm),
        in_specs=[
            pl.BlockSpec((None, tm, d), lambda bi, i: (bi, i, 0)),
            pl.BlockSpec((None, tm, SSM_WIDTH), lambda bi, i: (bi, i, 0)),
        ] + [_const_spec(c.shape) for c in consts],
        out_specs=pl.BlockSpec((None, tm, d), lambda bi, i: (bi, i, 0)),
        out_shape=jax.ShapeDtypeStruct((b, l, d), F32),
        scratch_shapes=[
            pltpu.VMEM((N_PAIRS, HIST + tm, PAIR), BF16),
            pltpu.VMEM((ATTN_WIDTH, HIST + tm), BF16),
            pltpu.VMEM((n_m, ATTN_WIDTH, WINDOW), BF16),
            pltpu.VMEM((N_PAIRS, n_m, unit_rows, PAIR), BF16),
            pltpu.VMEM((n_m, ATTN_WIDTH, PAIR), F32),
            pltpu.VMEM((WINDOW, unit_rows), F32), pltpu.VMEM((WINDOW, unit_rows), F32),
            pltpu.VMEM((WINDOW, unit_rows), BF16), pltpu.VMEM((WINDOW, unit_rows), BF16),
            pltpu.VMEM((1, unit_rows), F32), pltpu.VMEM((1, unit_rows), F32),
        ],
        compiler_params=pltpu.CompilerParams(
            dimension_semantics=("arbitrary", "arbitrary"), vmem_limit_bytes=VMEM_LIMIT_BYTES),
        name="main_block",
    )(x, s, *consts)


def _ssm_chunk_params(a_re, a_im, log_dt, b_re, b_im, c_re, c_im):
    hi = lax.Precision.HIGHEST
    n = SSM_CHUNK
    dt = jnp.exp(log_dt)[:, None]
    k = jnp.arange(n + 1, dtype=F32)[:, None, None]
    mag = jnp.exp(a_re * dt * k)
    pw_re = mag * jnp.cos(a_im * dt * k)
    pw_im = mag * jnp.sin(a_im * dt * k)
    n_re = pw_re[1] - 1.0
    n_im = pw_im[1]
    den = a_re * a_re + a_im * a_im
    f_re = ((n_re * a_re + n_im * a_im) / den)[..., None]
    f_im = ((n_im * a_re - n_re * a_im) / den)[..., None]
    bb_re = f_re * b_re - f_im * b_im
    bb_im = f_re * b_im + f_im * b_re
    w_re = pw_re[:n, :, :, None] * bb_re - pw_im[:n, :, :, None] * bb_im
    w_im = pw_re[:n, :, :, None] * bb_im + pw_im[:n, :, :, None] * bb_re
    kern = (jnp.einsum("gop,kgpc->gkco", c_re, w_re, precision=hi)
            - jnp.einsum("gop,kgpc->gkco", c_im, w_im, precision=hi))
    zeros = jnp.zeros_like(kern)
    m_rows = [jnp.concatenate([zeros[:, :sig], kern[:, :n - sig]], axis=1) for sig in range(n)]
    m = jnp.stack(m_rows, axis=1)
    m = m.transpose(0, 1, 3, 2, 4).reshape(SSM_GROUPS, LANE_TILE, LANE_TILE)
    ws_re = w_re[::-1].transpose(1, 0, 3, 2).reshape(SSM_GROUPS, LANE_TILE, SSM_STATE)
    ws_im = w_im[::-1].transpose(1, 0, 3, 2).reshape(SSM_GROUPS, LANE_TILE, SSM_STATE)
    mw = jnp.concatenate([m, ws_re, ws_im, ws_im, ws_re], axis=2).astype(BF16)
    cp_re = c_re[None] * pw_re[1:, :, None, :] - c_im[None] * pw_im[1:, :, None, :]
    cp_im = c_re[None] * pw_im[1:, :, None, :] + c_im[None] * pw_re[1:, :, None, :]
    p_re = cp_re.transpose(1, 3, 0, 2).reshape(SSM_GROUPS, SSM_STATE, LANE_TILE)
    p_im = cp_im.transpose(1, 3, 0, 2).reshape(SSM_GROUPS, SSM_STATE, LANE_TILE)
    p = jnp.concatenate([p_re, -p_im], axis=1).astype(BF16)
    aa = jnp.concatenate([pw_re[n], pw_re[n]], axis=1).reshape(1, -1)
    ab = jnp.concatenate([-pw_im[n], pw_im[n]], axis=1).reshape(1, -1)
    return mw, p, aa, ab


def _attn_bias_tables(rel_bias):
    n_rel = rel_bias.shape[1]
    ext = jnp.concatenate(
        [rel_bias, jnp.broadcast_to(rel_bias[:, -1:], (N_HEADS, CHUNK + BAND - 1 - n_rel))], axis=1)
    by_rev_k = jnp.stack([ext[:, q:q + BAND] for q in range(CHUNK)], axis=1)
    bias = by_rev_k[:, :, ::-1] * LOG2E
    masked = jnp.full((N_HEADS, CHUNK, CHUNK), NEG_INF, F32)
    even = jnp.concatenate([bias, masked], axis=2)
    odd = jnp.concatenate([masked, bias], axis=2)
    per_head = jnp.concatenate([even, odd], axis=1)
    return per_head.reshape(N_HEADS // 2, 4 * CHUNK, WINDOW).transpose(0, 2, 1)


def kernel(x, norm_gain, w_in, rel_bias, ssm_a_re, ssm_a_im, ssm_log_dt, ssm_b_re, ssm_b_im,
           ssm_c_re, ssm_c_im, ssm_d, w_glu, b_glu, w_attn_out, w_ssm_out, gate_bias, w_out,
           final_gain):
    assert norm_gain.shape[0] == 1, "single-layer block"
    b, l, d = x.shape
    aw, sw = ATTN_WIDTH, SSM_WIDTH
    w = w_in[0]
    wq = (w[:, :aw] * (LOG2E / math.sqrt(HEAD_DIM))).astype(BF16)
    wk = w[:, aw:2 * aw].astype(BF16)
    wvt = w[:, 2 * aw:3 * aw].T.astype(BF16)
    wzat = w[:, 3 * aw:4 * aw].T.astype(BF16)
    wzs = w[:, 4 * aw + sw:4 * aw + 2 * sw].astype(BF16)
    w_u = w[:, 4 * aw:4 * aw + sw].astype(BF16)
    wg = w[:, 4 * aw + 2 * sw:].astype(BF16)
    gain = norm_gain[0].reshape(1, d)

    mw, p, aa, ab = _ssm_chunk_params(
        ssm_a_re[0], ssm_a_im[0], ssm_log_dt[0], ssm_b_re[0], ssm_b_im[0], ssm_c_re[0], ssm_c_im[0])
    s = _ssm(x, [gain, w_u, mw, p, aa, ab, ssm_d[0].reshape(1, sw),
                 w_glu[0].astype(BF16), b_glu[0].reshape(1, 2 * sw)], steps=128)

    consts = [gain, wq, wk, wvt, wzat, wzs, wg, gate_bias[0].reshape(1, 2 * d),
              _attn_bias_tables(rel_bias[0]), w_attn_out[0].astype(BF16),
              w_ssm_out[0].astype(BF16), w_out[0].astype(BF16), final_gain.reshape(1, d)]
    return _main(x, s, consts)
```

```python
import functools
import math

import jax
import jax.numpy as jnp
from jax import lax
from jax.experimental import pallas as pl
from jax.experimental.pallas import tpu as pltpu

F32 = jnp.float32
BF16 = jnp.bfloat16

CHUNK = 64
LEFT_CHUNKS = 8
BAND = (LEFT_CHUNKS + 1) * CHUNK
HIST = LEFT_CHUNKS * CHUNK
N_HEADS = 8
HEAD_DIM = 64
ATTN_WIDTH = N_HEADS * HEAD_DIM
MAX_REL = 128
SSM_WIDTH = 512
SSM_GROUP = 16
SSM_GROUPS = SSM_WIDTH // SSM_GROUP
SSM_STATE = 64
NORM_EPS = 1e-6
NEG_INF = -1e30
LOG2E = math.log2(math.e)

VMEM_LIMIT_BYTES = 56 * 1024 * 1024


def _rms_norm(x, gain):
    inv = lax.rsqrt(jnp.mean(x * x, axis=-1, keepdims=True) + NORM_EPS)
    return x * inv * gain


def _const_spec(shape):
    zeros = (0,) * len(shape)
    return pl.BlockSpec(shape, lambda *_: zeros, pipeline_mode=pl.Buffered(1))


def _col_block_spec(rows, width, index):
    return pl.BlockSpec((rows, width), lambda *_: (0, index), pipeline_mode=pl.Buffered(1))


LANE_TILE = 128
SSM_CHUNK = LANE_TILE // SSM_GROUP
GROUPS_PER_TILE = LANE_TILE // SSM_GROUP
STATE_LANES = 2 * SSM_STATE


def _block_transpose(arrs):
    arrs = list(arrs)
    lane_block = lax.broadcasted_iota(jnp.int32, arrs[0].shape, 1) // SSM_GROUP
    for d in (4, 2, 1):
        low = (lane_block & d) == 0
        for lo in range(len(arrs)):
            if lo & d:
                continue
            a, b = arrs[lo], arrs[lo + d]
            arrs[lo] = jnp.where(low, a, pltpu.roll(b, SSM_GROUP * d, axis=1))
            arrs[lo + d] = jnp.where(low, pltpu.roll(a, LANE_TILE - SSM_GROUP * d, axis=1), b)
    return arrs


def _ssm_kernel(x_ref, gain_ref, wu_ref, m_ref, ws_ref, p_ref, aa_ref, ab_ref, d_ref, wglu_ref, bglu_ref,
                s_ref, slab_in, slab_out, ur_s, sc_s, scw_s, hp_s, st, st_sw,
                *, steps, batch):
    i = pl.program_id(0)
    n_chunks = steps // SSM_CHUNK
    rows_cb = n_chunks * batch
    pitch = steps + 8
    n_tiles = SSM_WIDTH // LANE_TILE

    @pl.when(i == 0)
    def _():
        st[...] = jnp.zeros_like(st)
        st_sw[...] = jnp.zeros_like(st_sw)

    x = x_ref[...].reshape(batch * steps, x_ref.shape[-1])
    h = _rms_norm(x, gain_ref[...]).astype(BF16)
    u_bt = jnp.dot(h, wu_ref[...], preferred_element_type=F32)

    for j in range(n_tiles):
        for b in range(batch):
            slab_in[j, pitch * b:pitch * b + steps, :] = (
                u_bt[steps * b:steps * (b + 1), LANE_TILE * j:LANE_TILE * (j + 1)])
    group_lanes = GROUPS_PER_TILE * STATE_LANES
    for j in range(n_tiles):
        lanes_j = slice(LANE_TILE * j, LANE_TILE * (j + 1))
        by_step = [
            jnp.concatenate([slab_in[j, pl.ds(SSM_CHUNK * c + tau, batch, stride=pitch), :]
                             for c in range(n_chunks)], axis=0)
            for tau in range(SSM_CHUNK)]
        for tau in range(SSM_CHUNK):
            ur_s[rows_cb * tau:rows_cb * (tau + 1), lanes_j] = by_step[tau]
        by_group = _block_transpose(by_step)

        intra = []
        for g8 in range(GROUPS_PER_TILE):
            g = GROUPS_PER_TILE * j + g8
            ug = by_group[g8].astype(BF16)
            intra.append(jnp.dot(ug, m_ref[g], preferred_element_type=F32))
            state = jnp.dot(ug, ws_ref[g], preferred_element_type=F32)
            sc_s[:, STATE_LANES * g:STATE_LANES * (g + 1)] = state[:, :STATE_LANES]
            scw_s[:, STATE_LANES * g:STATE_LANES * (g + 1)] = state[:, STATE_LANES:]

        sl = slice(group_lanes * j, group_lanes * (j + 1))
        aa = jnp.broadcast_to(aa_ref[:, sl], (batch, group_lanes))
        ab = jnp.broadcast_to(ab_ref[:, sl], (batch, group_lanes))
        hh, hw = st[:, sl], st_sw[:, sl]
        for c in range(n_chunks):
            rows = slice(batch * c, batch * (c + 1))
            hp_s[rows, sl] = hh
            hh, hw = (aa * hh + ab * hw + sc_s[rows, sl], aa * hw - ab * hh + scw_s[rows, sl])
        st[:, sl] = hh
        st_sw[:, sl] = hw

        by_group = []
        for g8 in range(GROUPS_PER_TILE):
            g = GROUPS_PER_TILE * j + g8
            hp = hp_s[:, STATE_LANES * g:STATE_LANES * (g + 1)].astype(BF16)
            by_group.append(intra[g8] + jnp.dot(hp, p_ref[g], preferred_element_type=F32))
        by_step = _block_transpose(by_group)
        for tau in range(SSM_CHUNK):
            rows = slice(rows_cb * tau, rows_cb * (tau + 1))
            ur_s[rows, lanes_j] = by_step[tau] + d_ref[:, lanes_j] * ur_s[rows, lanes_j]

    g_act = jax.nn.gelu(ur_s[...])
    z = jnp.dot(g_act.astype(BF16), wglu_ref[...], preferred_element_type=F32) + bglu_ref[...]
    s = z[:, :SSM_WIDTH] * jax.nn.sigmoid(z[:, SSM_WIDTH:])

    for tau in range(SSM_CHUNK):
        for c in range(n_chunks):
            r0 = rows_cb * tau + batch * c
            for j in range(n_tiles):
                slab_out[j, pl.ds(SSM_CHUNK * c + tau, batch, stride=pitch), :] = (
                    s[r0:r0 + batch, LANE_TILE * j:LANE_TILE * (j + 1)])
    for b in range(batch):
        s_ref[b] = jnp.concatenate(
            [slab_out[j, pitch * b:pitch * b + steps, :] for j in range(n_tiles)], axis=1)


def _ssm(x, w_all, consts, steps):
    batch, l, d = x.shape
    n_tiles = SSM_WIDTH // LANE_TILE
    rows_cb = steps // SSM_CHUNK * batch
    kern = functools.partial(_ssm_kernel, steps=steps, batch=batch)
    slab = pltpu.VMEM((n_tiles, (steps + 8) * batch, LANE_TILE), F32)
    state_cols = SSM_GROUPS * STATE_LANES
    return pl.pallas_call(
        kern,
        grid=(l // steps,),
        in_specs=[pl.BlockSpec((batch, steps, d), lambda i: (0, i, 0)), _const_spec(consts[0].shape),
                  _col_block_spec(d, SSM_WIDTH, U_COL_BLOCK)]
        + [_const_spec(c.shape) for c in consts[1:]],
        out_specs=pl.BlockSpec((batch, steps, SSM_WIDTH), lambda i: (0, i, 0)),
        out_shape=jax.ShapeDtypeStruct((batch, l, SSM_WIDTH), F32),
        scratch_shapes=[
            slab, slab,
            pltpu.VMEM((steps * batch, SSM_WIDTH), F32),
            pltpu.VMEM((rows_cb, state_cols), F32), pltpu.VMEM((rows_cb, state_cols), F32),
            pltpu.VMEM((rows_cb, state_cols), F32),
            pltpu.VMEM((batch, state_cols), F32), pltpu.VMEM((batch, state_cols), F32),
        ],
        compiler_params=pltpu.CompilerParams(
            dimension_semantics=("arbitrary",), vmem_limit_bytes=VMEM_LIMIT_BYTES),
        name="ssm",
    )(x, consts[0], w_all, *consts[1:])


PAIR = 2 * CHUNK
WINDOW = BAND + CHUNK
N_PAIRS = N_HEADS // 2
U_COL_BLOCK = 4
MAIN_W_IN_BLOCKS = ((ATTN_WIDTH, 0), (ATTN_WIDTH, 1), (ATTN_WIDTH, 2), (ATTN_WIDTH, 3),
                    (SSM_WIDTH, 5), (2 * ATTN_WIDTH, 3), (2 * ATTN_WIDTH, 4))


class _AttnRefs:
    def __init__(self, bias_ref, qm_s, k_s, vw_s, yat_s, s_bufs, p_bufs, r_bufs):
        self.bias_ref, self.qm_s, self.k_s, self.vw_s, self.yat_s = bias_ref, qm_s, k_s, vw_s, yat_s
        self.s_bufs, self.p_bufs, self.r_bufs = s_bufs, p_bufs, r_bufs


def _unit_index(k):
    if isinstance(k, int):
        return k // N_PAIRS, k % N_PAIRS
    return lax.shift_right_logical(k, 2), lax.bitwise_and(k, N_PAIRS - 1)


def _row_start(idx, size):
    return idx * size if isinstance(idx, int) else pl.multiple_of(idx * size, size)


def _nt_dot(a, b):
    return lax.dot_general(a, b, (((1,), (1,)), ((), ())), preferred_element_type=F32)


def _scores_stage(r, k, slot, masked_start):
    m, pr = _unit_index(k)
    kwin = r.k_s[pr, pl.ds(_row_start(m, PAIR), WINDOW), :]
    s = _nt_dot(kwin, r.qm_s[pr, m]) + r.bias_ref[pr]
    if masked_start:
        row = lax.broadcasted_iota(jnp.int32, s.shape, 0)
        s = jnp.where(row >= HIST - PAIR * m, s, NEG_INF)
    r.s_bufs[slot][...] = s


def _softmax_stage(r, slot):
    s = r.s_bufs[slot][...]
    p = jnp.exp2(s - jnp.max(s, axis=0, keepdims=True))
    r.r_bufs[slot][...] = 1.0 / jnp.sum(p, axis=0, keepdims=True)
    r.p_bufs[slot][...] = p.astype(BF16)


def _values_stage(r, k, slot):
    m, pr = _unit_index(k)
    vwin = r.vw_s[m, pl.ds(_row_start(pr, PAIR), PAIR), :]
    o = jnp.dot(vwin, r.p_bufs[slot][...], preferred_element_type=F32) * r.r_bufs[slot][...]
    r.yat_s[m, pl.ds(_row_start(pr, PAIR), PAIR), :] = jnp.concatenate(
        [o[:HEAD_DIM, :PAIR], o[HEAD_DIM:, PAIR:]], axis=0)


def _attention(r, n_units, masked_start):
    _scores_stage(r, 0, 0, masked_start)
    _scores_stage(r, 1, 1, masked_start)
    _softmax_stage(r, 0)

    def body(kk, carry):
        k = 2 * kk + 2
        _scores_stage(r, k, 0, masked_start)
        _softmax_stage(r, 1)
        _values_stage(r, k - 2, 0)
        _scores_stage(r, k + 1, 1, masked_start)
        _softmax_stage(r, 0)
        _values_stage(r, k - 1, 1)
        return carry

    lax.fori_loop(0, (n_units - 2) // 2, body, 0)
    _softmax_stage(r, 1)
    _values_stage(r, n_units - 2, 0)
    _values_stage(r, n_units - 1, 1)


def _main_kernel(x_ref, s_ref, gain_ref, wq_ref, wk_ref, wv_ref, wza_ref, wzs_ref, wga_ref, wgs_ref,
                 gb_ref, band_bias_ref, wao_ref, wso_ref, wo_ref, fgain_ref, o_ref,
                 k_s, vt_s, vw_s, qm_s, yat_s, s_buf0, s_buf1, p_buf0, p_buf1, r_buf0, r_buf1,
                 bias_s, *, tm):
    i = pl.program_id(1)
    n_m = tm // PAIR

    @pl.when((pl.program_id(0) == 0) & (i == 0))
    def _():
        masked = jnp.full((CHUNK, CHUNK), NEG_INF, F32)
        for pr in range(N_PAIRS):
            blocks = []
            for head in (2 * pr, 2 * pr + 1):
                band = band_bias_ref[head]
                blocks.append(jnp.concatenate([band, masked], axis=1))
                blocks.append(jnp.concatenate([masked, band], axis=1))
            bias_s[pr] = jnp.concatenate(blocks, axis=0).T

    x = x_ref[...]
    h = _rms_norm(x, gain_ref[...]).astype(BF16)

    q = jnp.dot(h, wq_ref[...], preferred_element_type=F32)
    lane = lax.broadcasted_iota(jnp.int32, (tm, PAIR), 1)
    for pr in range(N_PAIRS):
        qp = q[:, PAIR * pr:PAIR * (pr + 1)]
        lo = jnp.where(lane < HEAD_DIM, qp, 0.0).astype(BF16)
        hi = jnp.where(lane >= HEAD_DIM, qp, 0.0).astype(BF16)
        for m in range(n_m):
            qm_s[pr, m, 0:PAIR, :] = lo[PAIR * m:PAIR * (m + 1)]
            qm_s[pr, m, PAIR:2 * PAIR, :] = hi[PAIR * m:PAIR * (m + 1)]

    k_new = jnp.dot(h, wk_ref[...], preferred_element_type=F32)
    vt_new = jnp.dot(h, wv_ref[...], preferred_element_type=F32).T

    @pl.when(i == 0)
    def _():
        k_s[:, 0:HIST, :] = jnp.zeros((N_PAIRS, HIST, PAIR), BF16)
        vt_s[:, 0:HIST] = jnp.zeros((ATTN_WIDTH, HIST), BF16)

    @pl.when(i > 0)
    def _():
        k_s[:, 0:HIST, :] = k_s[:, tm:tm + HIST, :]
        vt_s[:, 0:HIST] = vt_s[:, tm:tm + HIST]

    vt_s[:, HIST:HIST + tm] = vt_new.astype(BF16)
    for pr in range(N_PAIRS):
        k_s[pr, HIST:HIST + tm, :] = k_new[:, PAIR * pr:PAIR * (pr + 1)].astype(BF16)
    for m in range(n_m):
        vw_s[m] = vt_s[:, PAIR * m:PAIR * m + WINDOW]

    refs = _AttnRefs(bias_s, qm_s, k_s, vw_s, yat_s,
                     (s_buf0, s_buf1), (p_buf0, p_buf1), (r_buf0, r_buf1))

    @pl.when(i == 0)
    def _():
        _attention(refs, n_m * N_PAIRS, True)

    @pl.when(i > 0)
    def _():
        _attention(refs, n_m * N_PAIRS, False)

    za = jnp.dot(h, wza_ref[...], preferred_element_type=F32)
    yat = jnp.concatenate([yat_s[m] for m in range(n_m)], axis=1)
    ya = yat.T * jax.nn.silu(za)
    yao = jnp.dot(ya.astype(BF16), wao_ref[...], preferred_element_type=F32)
    zs = jnp.dot(h, wzs_ref[...], preferred_element_type=F32)
    ys = s_ref[...] * jax.nn.silu(zs)
    yso = jnp.dot(ys.astype(BF16), wso_ref[...], preferred_element_type=F32)
    d = x.shape[-1]
    ga = jnp.dot(h, wga_ref[...], preferred_element_type=F32) + gb_ref[:, :d]
    gs = jnp.dot(h, wgs_ref[...], preferred_element_type=F32) + gb_ref[:, d:]
    merged = jax.nn.sigmoid(ga) * yao + jax.nn.sigmoid(gs) * yso
    xn = x + jnp.dot(merged.astype(BF16), wo_ref[...], preferred_element_type=F32)
    o_ref[...] = _rms_norm(xn, fgain_ref[...])


def _main(x, s, w_all, gain, consts):
    b, l, d = x.shape
    tm = HIST
    n_m = tm // PAIR
    kern = functools.partial(_main_kernel, tm=tm)
    unit_rows = 2 * PAIR
    return pl.pallas_call(
        kern,
        grid=(b, l // tm),
        in_specs=[
            pl.BlockSpec((None, tm, d), lambda bi, i: (bi, i, 0)),
            pl.BlockSpec((None, tm, SSM_WIDTH), lambda bi, i: (bi, i, 0)),
            _const_spec(gain.shape),
        ] + [_col_block_spec(d, width, index) for width, index in MAIN_W_IN_BLOCKS]
        + [_const_spec(c.shape) for c in consts],
        out_specs=pl.BlockSpec((None, tm, d), lambda bi, i: (bi, i, 0)),
        out_shape=jax.ShapeDtypeStruct((b, l, d), F32),
        scratch_shapes=[
            pltpu.VMEM((N_PAIRS, HIST + tm, PAIR), BF16),
            pltpu.VMEM((ATTN_WIDTH, HIST + tm), BF16),
            pltpu.VMEM((n_m, ATTN_WIDTH, WINDOW), BF16),
            pltpu.VMEM((N_PAIRS, n_m, unit_rows, PAIR), BF16),
            pltpu.VMEM((n_m, ATTN_WIDTH, PAIR), F32),
            pltpu.VMEM((WINDOW, unit_rows), F32), pltpu.VMEM((WINDOW, unit_rows), F32),
            pltpu.VMEM((WINDOW, unit_rows), BF16), pltpu.VMEM((WINDOW, unit_rows), BF16),
            pltpu.VMEM((1, unit_rows), F32), pltpu.VMEM((1, unit_rows), F32),
            pltpu.VMEM((N_PAIRS, WINDOW, unit_rows), F32),
        ],
        compiler_params=pltpu.CompilerParams(
            dimension_semantics=("arbitrary", "arbitrary"), vmem_limit_bytes=VMEM_LIMIT_BYTES),
        name="main_block",
    )(x, s, gain, *([w_all] * len(MAIN_W_IN_BLOCKS)), *consts)


def _ssm_chunk_params(a_re, a_im, log_dt, b_re, b_im, c_re, c_im):
    hi = lax.Precision.HIGHEST
    n = SSM_CHUNK
    dt = jnp.exp(log_dt)[:, None]
    k = jnp.arange(n + 1, dtype=F32)[None, :, None]
    mag = jnp.exp((a_re * dt)[:, None, :] * k)
    pw_re = mag * jnp.cos((a_im * dt)[:, None, :] * k)
    pw_im = mag * jnp.sin((a_im * dt)[:, None, :] * k)
    n_re = pw_re[:, 1] - 1.0
    n_im = pw_im[:, 1]
    den = a_re * a_re + a_im * a_im
    f_re = ((n_re * a_re + n_im * a_im) / den)[:, None, :]
    f_im = ((n_im * a_re - n_re * a_im) / den)[:, None, :]
    bt_re, bt_im = jnp.swapaxes(b_re, 1, 2), jnp.swapaxes(b_im, 1, 2)
    bb_re = f_re * bt_re - f_im * bt_im
    bb_im = f_re * bt_im + f_im * bt_re
    w_re = pw_re[:, :n, None, :] * bb_re[:, None] - pw_im[:, :n, None, :] * bb_im[:, None]
    w_im = pw_re[:, :n, None, :] * bb_im[:, None] + pw_im[:, :n, None, :] * bb_re[:, None]
    kern = (jnp.einsum("gop,gkcp->gcko", c_re, w_re, precision=hi)
            - jnp.einsum("gop,gkcp->gcko", c_im, w_im, precision=hi))
    kern = kern.reshape(SSM_GROUPS, SSM_GROUP, LANE_TILE)
    zeros = jnp.zeros_like(kern)
    m = jnp.stack([jnp.concatenate([zeros[:, :, :SSM_GROUP * sig], kern[:, :, :SSM_GROUP * (n - sig)]],
                                   axis=2) for sig in range(n)], axis=1)
    m = m.reshape(SSM_GROUPS, LANE_TILE, LANE_TILE).astype(BF16)
    ws_re = w_re[:, ::-1].reshape(SSM_GROUPS, LANE_TILE, SSM_STATE)
    ws_im = w_im[:, ::-1].reshape(SSM_GROUPS, LANE_TILE, SSM_STATE)
    ws = jnp.concatenate([ws_re, ws_im, ws_im, ws_re], axis=2).astype(BF16)
    ct_re, ct_im = jnp.swapaxes(c_re, 1, 2)[:, :, None, :], jnp.swapaxes(c_im, 1, 2)[:, :, None, :]
    pt_re = jnp.swapaxes(pw_re[:, 1:], 1, 2)[..., None]
    pt_im = jnp.swapaxes(pw_im[:, 1:], 1, 2)[..., None]
    p_re = (ct_re * pt_re - ct_im * pt_im).reshape(SSM_GROUPS, SSM_STATE, LANE_TILE)
    p_im = (ct_re * pt_im + ct_im * pt_re).reshape(SSM_GROUPS, SSM_STATE, LANE_TILE)
    p = jnp.concatenate([p_re, -p_im], axis=1).astype(BF16)
    aa = jnp.concatenate([pw_re[:, n], pw_re[:, n]], axis=1).reshape(1, -1)
    ab = jnp.concatenate([-pw_im[:, n], pw_im[:, n]], axis=1).reshape(1, -1)
    return m, ws, p, aa, ab


def _band_bias(rel_bias):
    n_rel = rel_bias.shape[1]
    ext = jnp.concatenate(
        [rel_bias, jnp.broadcast_to(rel_bias[:, -1:], (N_HEADS, CHUNK + BAND - 1 - n_rel))], axis=1)
    by_rev_k = jnp.stack([ext[:, q:q + BAND] for q in range(CHUNK)], axis=1)
    return by_rev_k[:, :, ::-1] * LOG2E


def kernel(x, norm_gain, w_in, rel_bias, ssm_a_re, ssm_a_im, ssm_log_dt, ssm_b_re, ssm_b_im,
           ssm_c_re, ssm_c_im, ssm_d, w_glu, b_glu, w_attn_out, w_ssm_out, gate_bias, w_out,
           final_gain):
    assert norm_gain.shape[0] == 1, "single-layer block"
    b, l, d = x.shape
    aw, sw = ATTN_WIDTH, SSM_WIDTH
    col_scale = jnp.where(jnp.arange(w_in.shape[-1]) < aw, LOG2E / math.sqrt(HEAD_DIM), 1.0)
    w_all = (w_in[0] * col_scale).astype(BF16)
    gain = norm_gain[0].reshape(1, d)

    m, ws, p, aa, ab = _ssm_chunk_params(
        ssm_a_re[0], ssm_a_im[0], ssm_log_dt[0], ssm_b_re[0], ssm_b_im[0], ssm_c_re[0], ssm_c_im[0])
    s = _ssm(x, w_all, [gain, m, ws, p, aa, ab, ssm_d[0].reshape(1, sw),
                        w_glu[0].astype(BF16), b_glu[0].reshape(1, 2 * sw)], steps=128)

    consts = [gate_bias[0].reshape(1, 2 * d), _band_bias(rel_bias[0]),
              w_attn_out[0].astype(BF16), w_ssm_out[0].astype(BF16), w_out[0].astype(BF16),
              final_gain.reshape(1, d)]
    return _main(x, s, w_all, gain, consts)
```

```python
import functools
import math

import jax
import jax.numpy as jnp
from jax import lax
from jax.experimental import pallas as pl
from jax.experimental.pallas import tpu as pltpu

F32 = jnp.float32
BF16 = jnp.bfloat16

CHUNK = 64
LEFT_CHUNKS = 8
BAND = (LEFT_CHUNKS + 1) * CHUNK
HIST = LEFT_CHUNKS * CHUNK
N_HEADS = 8
HEAD_DIM = 64
ATTN_WIDTH = N_HEADS * HEAD_DIM
MAX_REL = 128
SSM_WIDTH = 512
SSM_GROUP = 16
SSM_GROUPS = SSM_WIDTH // SSM_GROUP
SSM_STATE = 64
NORM_EPS = 1e-6
NEG_INF = -1e30
LOG2E = math.log2(math.e)

VMEM_LIMIT_BYTES = 56 * 1024 * 1024


def _rms_norm(x, gain):
    inv = lax.rsqrt(jnp.mean(x * x, axis=-1, keepdims=True) + NORM_EPS)
    return x * inv * gain


def _const_spec(shape):
    zeros = (0,) * len(shape)
    return pl.BlockSpec(shape, lambda *_: zeros, pipeline_mode=pl.Buffered(1))


def _col_block_spec(rows, width, index):
    return pl.BlockSpec((rows, width), lambda *_: (0, index), pipeline_mode=pl.Buffered(1))


LANE_TILE = 128
SSM_CHUNK = LANE_TILE // SSM_GROUP
GROUPS_PER_TILE = LANE_TILE // SSM_GROUP
STATE_LANES = 2 * SSM_STATE


def _block_transpose(arrs):
    arrs = list(arrs)
    lane_block = lax.broadcasted_iota(jnp.int32, arrs[0].shape, 1) // SSM_GROUP
    for d in (4, 2, 1):
        low = (lane_block & d) == 0
        for lo in range(len(arrs)):
            if lo & d:
                continue
            a, b = arrs[lo], arrs[lo + d]
            arrs[lo] = jnp.where(low, a, pltpu.roll(b, SSM_GROUP * d, axis=1))
            arrs[lo + d] = jnp.where(low, pltpu.roll(a, LANE_TILE - SSM_GROUP * d, axis=1), b)
    return arrs


def _ssm_kernel(x_ref, gain_ref, wu_ref, m_ref, ws_ref, p_ref, aa_ref, ab_ref, d_ref, wglu_ref, bglu_ref,
                s_ref, h_ref, slab_in, slab_out, ur_s, sc_s, scw_s, hp_s, st, st_sw,
                *, steps, batch):
    i = pl.program_id(0)
    n_chunks = steps // SSM_CHUNK
    rows_cb = n_chunks * batch
    pitch = steps + 8
    n_tiles = SSM_WIDTH // LANE_TILE

    @pl.when(i == 0)
    def _():
        st[...] = jnp.zeros_like(st)
        st_sw[...] = jnp.zeros_like(st_sw)

    x = x_ref[...].reshape(batch * steps, x_ref.shape[-1])
    h = _rms_norm(x, gain_ref[...]).astype(BF16)
    h_ref[...] = h.reshape(h_ref.shape)
    u_bt = jnp.dot(h, wu_ref[...], preferred_element_type=F32)

    for j in range(n_tiles):
        for b in range(batch):
            slab_in[j, pitch * b:pitch * b + steps, :] = (
                u_bt[steps * b:steps * (b + 1), LANE_TILE * j:LANE_TILE * (j + 1)])
    group_lanes = GROUPS_PER_TILE * STATE_LANES
    for j in range(n_tiles):
        lanes_j = slice(LANE_TILE * j, LANE_TILE * (j + 1))
        by_step = [
            jnp.concatenate([slab_in[j, pl.ds(SSM_CHUNK * c + tau, batch, stride=pitch), :]
                             for c in range(n_chunks)], axis=0)
            for tau in range(SSM_CHUNK)]
        for tau in range(SSM_CHUNK):
            ur_s[rows_cb * tau:rows_cb * (tau + 1), lanes_j] = by_step[tau]
        by_group = _block_transpose(by_step)

        intra = []
        for g8 in range(GROUPS_PER_TILE):
            g = GROUPS_PER_TILE * j + g8
            ug = by_group[g8].astype(BF16)
            intra.append(jnp.dot(ug, m_ref[g], preferred_element_type=F32))
            state = jnp.dot(ug, ws_ref[g], preferred_element_type=F32)
            sc_s[:, STATE_LANES * g:STATE_LANES * (g + 1)] = state[:, :STATE_LANES]
            scw_s[:, STATE_LANES * g:STATE_LANES * (g + 1)] = state[:, STATE_LANES:]

        sl = slice(group_lanes * j, group_lanes * (j + 1))
        aa = jnp.broadcast_to(aa_ref[:, sl], (batch, group_lanes))
        ab = jnp.broadcast_to(ab_ref[:, sl], (batch, group_lanes))
        hh, hw = st[:, sl], st_sw[:, sl]
        for c in range(n_chunks):
            rows = slice(batch * c, batch * (c + 1))
            hp_s[rows, sl] = hh
            hh, hw = (aa * hh + ab * hw + sc_s[rows, sl], aa * hw - ab * hh + scw_s[rows, sl])
        st[:, sl] = hh
        st_sw[:, sl] = hw

        by_group = []
        for g8 in range(GROUPS_PER_TILE):
            g = GROUPS_PER_TILE * j + g8
            hp = hp_s[:, STATE_LANES * g:STATE_LANES * (g + 1)].astype(BF16)
            by_group.append(intra[g8] + jnp.dot(hp, p_ref[g], preferred_element_type=F32))
        by_step = _block_transpose(by_group)
        for tau in range(SSM_CHUNK):
            rows = slice(rows_cb * tau, rows_cb * (tau + 1))
            ur_s[rows, lanes_j] = by_step[tau] + d_ref[:, lanes_j] * ur_s[rows, lanes_j]

    g_act = jax.nn.gelu(ur_s[...])
    z = jnp.dot(g_act.astype(BF16), wglu_ref[...], preferred_element_type=F32) + bglu_ref[...]
    s = z[:, :SSM_WIDTH] * jax.nn.sigmoid(z[:, SSM_WIDTH:])

    for tau in range(SSM_CHUNK):
        for c in range(n_chunks):
            r0 = rows_cb * tau + batch * c
            for j in range(n_tiles):
                slab_out[j, pl.ds(SSM_CHUNK * c + tau, batch, stride=pitch), :] = (
                    s[r0:r0 + batch, LANE_TILE * j:LANE_TILE * (j + 1)])
    for b in range(batch):
        s_ref[b] = jnp.concatenate(
            [slab_out[j, pitch * b:pitch * b + steps, :] for j in range(n_tiles)], axis=1)


def _ssm(x, w_all, consts, steps):
    batch, l, d = x.shape
    n_tiles = SSM_WIDTH // LANE_TILE
    rows_cb = steps // SSM_CHUNK * batch
    kern = functools.partial(_ssm_kernel, steps=steps, batch=batch)
    slab = pltpu.VMEM((n_tiles, (steps + 8) * batch, LANE_TILE), F32)
    state_cols = SSM_GROUPS * STATE_LANES
    return pl.pallas_call(
        kern,
        grid=(l // steps,),
        in_specs=[pl.BlockSpec((batch, steps, d), lambda i: (0, i, 0)), _const_spec(consts[0].shape),
                  _col_block_spec(d, SSM_WIDTH, U_COL_BLOCK)]
        + [_const_spec(c.shape) for c in consts[1:]],
        out_specs=[pl.BlockSpec((batch, steps, SSM_WIDTH), lambda i: (0, i, 0)),
                   pl.BlockSpec((batch, steps, d), lambda i: (0, i, 0))],
        out_shape=[jax.ShapeDtypeStruct((batch, l, SSM_WIDTH), F32),
                   jax.ShapeDtypeStruct((batch, l, d), BF16)],
        scratch_shapes=[
            slab, slab,
            pltpu.VMEM((steps * batch, SSM_WIDTH), F32),
            pltpu.VMEM((rows_cb, state_cols), F32), pltpu.VMEM((rows_cb, state_cols), F32),
            pltpu.VMEM((rows_cb, state_cols), F32),
            pltpu.VMEM((batch, state_cols), F32), pltpu.VMEM((batch, state_cols), F32),
        ],
        compiler_params=pltpu.CompilerParams(
            dimension_semantics=("arbitrary",), vmem_limit_bytes=VMEM_LIMIT_BYTES),
        name="ssm",
    )(x, consts[0], w_all, *consts[1:])


PAIR = 2 * CHUNK
WINDOW = BAND + CHUNK
N_PAIRS = N_HEADS // 2
U_COL_BLOCK = 4
MAIN_W_IN_BLOCKS = ((ATTN_WIDTH, 0), (ATTN_WIDTH, 1), (ATTN_WIDTH, 2), (ATTN_WIDTH, 3),
                    (SSM_WIDTH, 5), (2 * ATTN_WIDTH, 3), (2 * ATTN_WIDTH, 4))


class _AttnRefs:
    def __init__(self, bias_ref, qm_s, k_s, vt_s, yat_s, s_bufs, p_bufs, r_bufs):
        self.bias_ref, self.qm_s, self.k_s, self.vt_s, self.yat_s = bias_ref, qm_s, k_s, vt_s, yat_s
        self.s_bufs, self.p_bufs, self.r_bufs = s_bufs, p_bufs, r_bufs


def _unit_index(k):
    if isinstance(k, int):
        return k // N_PAIRS, k % N_PAIRS
    return lax.shift_right_logical(k, 2), lax.bitwise_and(k, N_PAIRS - 1)


def _row_start(idx, size):
    return idx * size if isinstance(idx, int) else pl.multiple_of(idx * size, size)


def _nt_dot(a, b):
    return lax.dot_general(a, b, (((1,), (1,)), ((), ())), preferred_element_type=F32)


def _scores_stage(r, k, slot, masked_start):
    m, pr = _unit_index(k)
    kwin = r.k_s[pr, pl.ds(_row_start(m, PAIR), WINDOW), :]
    s = _nt_dot(kwin, r.qm_s[pr, m]) + r.bias_ref[pr]
    if masked_start:
        row = lax.broadcasted_iota(jnp.int32, s.shape, 0)
        s = jnp.where(row >= HIST - PAIR * m, s, NEG_INF)
    r.s_bufs[slot][...] = s


def _softmax_stage(r, slot):
    s = r.s_bufs[slot][...]
    p = jnp.exp2(s - jnp.max(s, axis=0, keepdims=True))
    r.r_bufs[slot][...] = 1.0 / jnp.sum(p, axis=0, keepdims=True)
    r.p_bufs[slot][...] = p.astype(BF16)


def _values_stage(r, k, slot):
    m, pr = _unit_index(k)
    vwin = r.vt_s[pl.ds(_row_start(pr, PAIR), PAIR), pl.ds(_row_start(m, PAIR), WINDOW)]
    o = jnp.dot(vwin, r.p_bufs[slot][...], preferred_element_type=F32) * r.r_bufs[slot][...]
    r.yat_s[m, pl.ds(_row_start(pr, PAIR), PAIR), :] = jnp.concatenate(
        [o[:HEAD_DIM, :PAIR], o[HEAD_DIM:, PAIR:]], axis=0)


def _attention(r, n_units, masked_start):
    _scores_stage(r, 0, 0, masked_start)
    _scores_stage(r, 1, 1, masked_start)
    _softmax_stage(r, 0)

    def body(kk, carry):
        k = 2 * kk + 2
        _scores_stage(r, k, 0, masked_start)
        _softmax_stage(r, 1)
        _values_stage(r, k - 2, 0)
        _scores_stage(r, k + 1, 1, masked_start)
        _softmax_stage(r, 0)
        _values_stage(r, k - 1, 1)
        return carry

    lax.fori_loop(0, (n_units - 2) // 2, body, 0)
    _softmax_stage(r, 1)
    _values_stage(r, n_units - 2, 0)
    _values_stage(r, n_units - 1, 1)


def _main_kernel(x_ref, h_ref, s_ref, wq_ref, wk_ref, wv_ref, wza_ref, wzs_ref, wga_ref, wgs_ref,
                 gb_ref, band_bias_ref, wao_ref, wso_ref, wo_ref, fgain_ref, o_ref,
                 k_s, vt_s, qm_s, yat_s, s_buf0, s_buf1, p_buf0, p_buf1, r_buf0, r_buf1,
                 bias_s, *, tm):
    i = pl.program_id(1)
    n_m = tm // PAIR

    @pl.when((pl.program_id(0) == 0) & (i == 0))
    def _():
        masked = jnp.full((CHUNK, CHUNK), NEG_INF, F32)
        for pr in range(N_PAIRS):
            blocks = []
            for head in (2 * pr, 2 * pr + 1):
                band = band_bias_ref[head]
                blocks.append(jnp.concatenate([band, masked], axis=1))
                blocks.append(jnp.concatenate([masked, band], axis=1))
            bias_s[pr] = jnp.concatenate(blocks, axis=0).T

    h = h_ref[...]

    q = jnp.dot(h, wq_ref[...], preferred_element_type=F32)
    lane = lax.broadcasted_iota(jnp.int32, (tm, PAIR), 1)
    for pr in range(N_PAIRS):
        qp = q[:, PAIR * pr:PAIR * (pr + 1)]
        lo = jnp.where(lane < HEAD_DIM, qp, 0.0).astype(BF16)
        hi = jnp.where(lane >= HEAD_DIM, qp, 0.0).astype(BF16)
        for m in range(n_m):
            qm_s[pr, m, 0:PAIR, :] = lo[PAIR * m:PAIR * (m + 1)]
            qm_s[pr, m, PAIR:2 * PAIR, :] = hi[PAIR * m:PAIR * (m + 1)]

    k_new = jnp.dot(h, wk_ref[...], preferred_element_type=F32)
    vt_new = jnp.dot(h, wv_ref[...], preferred_element_type=F32).T

    @pl.when(i == 0)
    def _():
        k_s[:, 0:HIST, :] = jnp.zeros((N_PAIRS, HIST, PAIR), BF16)
        vt_s[:, 0:HIST] = jnp.zeros((ATTN_WIDTH, HIST), BF16)

    @pl.when(i > 0)
    def _():
        k_s[:, 0:HIST, :] = k_s[:, tm:tm + HIST, :]
        vt_s[:, 0:HIST] = vt_s[:, tm:tm + HIST]

    vt_s[:, HIST:HIST + tm] = vt_new.astype(BF16)
    for pr in range(N_PAIRS):
        k_s[pr, HIST:HIST + tm, :] = k_new[:, PAIR * pr:PAIR * (pr + 1)].astype(BF16)

    refs = _AttnRefs(bias_s, qm_s, k_s, vt_s, yat_s,
                     (s_buf0, s_buf1), (p_buf0, p_buf1), (r_buf0, r_buf1))

    @pl.when(i == 0)
    def _():
        _attention(refs, n_m * N_PAIRS, True)

    @pl.when(i > 0)
    def _():
        _attention(refs, n_m * N_PAIRS, False)

    za = jnp.dot(h, wza_ref[...], preferred_element_type=F32)
    yat = jnp.concatenate([yat_s[m] for m in range(n_m)], axis=1)
    ya = yat.T * jax.nn.silu(za)
    yao = jnp.dot(ya.astype(BF16), wao_ref[...], preferred_element_type=F32)
    zs = jnp.dot(h, wzs_ref[...], preferred_element_type=F32)
    ys = s_ref[...] * jax.nn.silu(zs)
    yso = jnp.dot(ys.astype(BF16), wso_ref[...], preferred_element_type=F32)
    d = x_ref.shape[-1]
    ga = jnp.dot(h, wga_ref[...], preferred_element_type=F32) + gb_ref[:, :d]
    gs = jnp.dot(h, wgs_ref[...], preferred_element_type=F32) + gb_ref[:, d:]
    merged = jax.nn.sigmoid(ga) * yao + jax.nn.sigmoid(gs) * yso
    xn = x_ref[...] + jnp.dot(merged.astype(BF16), wo_ref[...], preferred_element_type=F32)
    o_ref[...] = _rms_norm(xn, fgain_ref[...])


def _main(x, h, s, w_all, consts):
    b, l, d = x.shape
    tm = HIST
    n_m = tm // PAIR
    kern = functools.partial(_main_kernel, tm=tm)
    unit_rows = 2 * PAIR
    return pl.pallas_call(
        kern,
        grid=(b, l // tm),
        in_specs=[
            pl.BlockSpec((None, tm, d), lambda bi, i: (bi, i, 0)),
            pl.BlockSpec((None, tm, d), lambda bi, i: (bi, i, 0)),
            pl.BlockSpec((None, tm, SSM_WIDTH), lambda bi, i: (bi, i, 0)),
        ] + [_col_block_spec(d, width, index) for width, index in MAIN_W_IN_BLOCKS]
        + [_const_spec(c.shape) for c in consts],
        out_specs=pl.BlockSpec((None, tm, d), lambda bi, i: (bi, i, 0)),
        out_shape=jax.ShapeDtypeStruct((b, l, d), F32),
        scratch_shapes=[
            pltpu.VMEM((N_PAIRS, HIST + tm, PAIR), BF16),
            pltpu.VMEM((ATTN_WIDTH, HIST + tm), BF16),
            pltpu.VMEM((N_PAIRS, n_m, unit_rows, PAIR), BF16),
            pltpu.VMEM((n_m, ATTN_WIDTH, PAIR), F32),
            pltpu.VMEM((WINDOW, unit_rows), F32), pltpu.VMEM((WINDOW, unit_rows), F32),
            pltpu.VMEM((WINDOW, unit_rows), BF16), pltpu.VMEM((WINDOW, unit_rows), BF16),
            pltpu.VMEM((1, unit_rows), F32), pltpu.VMEM((1, unit_rows), F32),
            pltpu.VMEM((N_PAIRS, WINDOW, unit_rows), F32),
        ],
        compiler_params=pltpu.CompilerParams(
            dimension_semantics=("arbitrary", "arbitrary"), vmem_limit_bytes=VMEM_LIMIT_BYTES),
        name="main_block",
    )(x, h, s, *([w_all] * len(MAIN_W_IN_BLOCKS)), *consts)


def _ssm_chunk_params(a_re, a_im, log_dt, b_re, b_im, c_re, c_im):
    hi = lax.Precision.HIGHEST
    n = SSM_CHUNK
    dt = jnp.exp(log_dt)[:, None]
    k = jnp.arange(n + 1, dtype=F32)[None, :, None]
    mag = jnp.exp((a_re * dt)[:, None, :] * k)
    pw_re = mag * jnp.cos((a_im * dt)[:, None, :] * k)
    pw_im = mag * jnp.sin((a_im * dt)[:, None, :] * k)
    n_re = pw_re[:, 1] - 1.0
    n_im = pw_im[:, 1]
    den = a_re * a_re + a_im * a_im
    f_re = ((n_re * a_re + n_im * a_im) / den)[:, None, :]
    f_im = ((n_im * a_re - n_re * a_im) / den)[:, None, :]
    bt_re, bt_im = jnp.swapaxes(b_re, 1, 2), jnp.swapaxes(b_im, 1, 2)
    bb_re = f_re * bt_re - f_im * bt_im
    bb_im = f_re * bt_im + f_im * bt_re
    w_re = pw_re[:, :n, None, :] * bb_re[:, None] - pw_im[:, :n, None, :] * bb_im[:, None]
    w_im = pw_re[:, :n, None, :] * bb_im[:, None] + pw_im[:, :n, None, :] * bb_re[:, None]
    kern = (jnp.einsum("gop,gkcp->gcko", c_re, w_re, precision=hi)
            - jnp.einsum("gop,gkcp->gcko", c_im, w_im, precision=hi))
    kern = kern.reshape(SSM_GROUPS, SSM_GROUP, LANE_TILE)
    zeros = jnp.zeros_like(kern)
    m = jnp.stack([jnp.concatenate([zeros[:, :, :SSM_GROUP * sig], kern[:, :, :SSM_GROUP * (n - sig)]],
                                   axis=2) for sig in range(n)], axis=1)
    m = m.reshape(SSM_GROUPS, LANE_TILE, LANE_TILE).astype(BF16)
    ws_re = w_re[:, ::-1].reshape(SSM_GROUPS, LANE_TILE, SSM_STATE)
    ws_im = w_im[:, ::-1].reshape(SSM_GROUPS, LANE_TILE, SSM_STATE)
    ws = jnp.concatenate([ws_re, ws_im, ws_im, ws_re], axis=2).astype(BF16)
    ct_re, ct_im = jnp.swapaxes(c_re, 1, 2)[:, :, None, :], jnp.swapaxes(c_im, 1, 2)[:, :, None, :]
    pt_re = jnp.swapaxes(pw_re[:, 1:], 1, 2)[..., None]
    pt_im = jnp.swapaxes(pw_im[:, 1:], 1, 2)[..., None]
    p_re = (ct_re * pt_re - ct_im * pt_im).reshape(SSM_GROUPS, SSM_STATE, LANE_TILE)
    p_im = (ct_re * pt_im + ct_im * pt_re).reshape(SSM_GROUPS, SSM_STATE, LANE_TILE)
    p = jnp.concatenate([p_re, -p_im], axis=1).astype(BF16)
    aa = jnp.concatenate([pw_re[:, n], pw_re[:, n]], axis=1).reshape(1, -1)
    ab = jnp.concatenate([-pw_im[:, n], pw_im[:, n]], axis=1).reshape(1, -1)
    return m, ws, p, aa, ab


def _band_bias(rel_bias):
    n_rel = rel_bias.shape[1]
    ext = jnp.concatenate(
        [rel_bias, jnp.broadcast_to(rel_bias[:, -1:], (N_HEADS, CHUNK + BAND - 1 - n_rel))], axis=1)
    by_rev_k = jnp.stack([ext[:, q:q + BAND] for q in range(CHUNK)], axis=1)
    return by_rev_k[:, :, ::-1] * LOG2E


def kernel(x, norm_gain, w_in, rel_bias, ssm_a_re, ssm_a_im, ssm_log_dt, ssm_b_re, ssm_b_im,
           ssm_c_re, ssm_c_im, ssm_d, w_glu, b_glu, w_attn_out, w_ssm_out, gate_bias, w_out,
           final_gain):
    assert norm_gain.shape[0] == 1, "single-layer block"
    b, l, d = x.shape
    aw, sw = ATTN_WIDTH, SSM_WIDTH
    col_scale = jnp.where(jnp.arange(w_in.shape[-1]) < aw, LOG2E / math.sqrt(HEAD_DIM), 1.0)
    w_all = (w_in[0] * col_scale).astype(BF16)
    gain = norm_gain[0].reshape(1, d)

    m, ws, p, aa, ab = _ssm_chunk_params(
        ssm_a_re[0], ssm_a_im[0], ssm_log_dt[0], ssm_b_re[0], ssm_b_im[0], ssm_c_re[0], ssm_c_im[0])
    s, h = _ssm(x, w_all, [gain, m, ws, p, aa, ab, ssm_d[0].reshape(1, sw),
                        w_glu[0].astype(BF16), b_glu[0].reshape(1, 2 * sw)], steps=128)

    consts = [gate_bias[0].reshape(1, 2 * d), _band_bias(rel_bias[0]),
              w_attn_out[0].astype(BF16), w_ssm_out[0].astype(BF16), w_out[0].astype(BF16),
              final_gain.reshape(1, d)]
    return _main(x, h, s, w_all, consts)
```

```python
import functools
import math

import jax
import jax.numpy as jnp
from jax import lax
from jax.experimental import pallas as pl
from jax.experimental.pallas import tpu as pltpu

F32 = jnp.float32
BF16 = jnp.bfloat16

CHUNK = 64
LEFT_CHUNKS = 8
BAND = (LEFT_CHUNKS + 1) * CHUNK
HIST = LEFT_CHUNKS * CHUNK
N_HEADS = 8
HEAD_DIM = 64
ATTN_WIDTH = N_HEADS * HEAD_DIM
SSM_WIDTH = 512
SSM_GROUP = 16
SSM_GROUPS = SSM_WIDTH // SSM_GROUP
SSM_STATE = 64
NORM_EPS = 1e-6
NEG_INF = -1e30
LOG2E = math.log2(math.e)

V7X_VMEM_BYTES = 64 * 1024 * 1024
VMEM_LIMIT_BYTES = V7X_VMEM_BYTES - 8 * 1024 * 1024
SSM_STEPS = 128
MAIN_TILE = HIST


def _rms_norm(x, gain):
    inv = lax.rsqrt(jnp.mean(x * x, axis=-1, keepdims=True) + NORM_EPS)
    return x * inv * gain


def _const_spec(shape):
    zeros = (0,) * len(shape)
    return pl.BlockSpec(shape, lambda *_: zeros, pipeline_mode=pl.Buffered(1))


def _col_block_spec(rows, width, index):
    return pl.BlockSpec((rows, width), lambda *_: (0, index), pipeline_mode=pl.Buffered(1))


LANE_TILE = 128
SSM_CHUNK = LANE_TILE // SSM_GROUP
GROUPS_PER_TILE = LANE_TILE // SSM_GROUP
STATE_LANES = 2 * SSM_STATE


def _block_transpose(arrs):
    arrs = list(arrs)
    lane_block = lax.broadcasted_iota(jnp.int32, arrs[0].shape, 1) // SSM_GROUP
    for d in (4, 2, 1):
        low = (lane_block & d) == 0
        for lo in range(len(arrs)):
            if lo & d:
                continue
            a, b = arrs[lo], arrs[lo + d]
            arrs[lo] = jnp.where(low, a, pltpu.roll(b, SSM_GROUP * d, axis=1))
            arrs[lo + d] = jnp.where(low, pltpu.roll(a, LANE_TILE - SSM_GROUP * d, axis=1), b)
    return arrs


def _ssm_kernel(x_ref, gain_ref, wu_ref, m_ref, ws_ref, p_ref, aa_ref, ab_ref, d_ref, wglu_ref, bglu_ref,
                s_ref, h_ref, slab_in, slab_out, ur_s, sc_s, scw_s, hp_s, st, st_sw,
                *, steps, batch):
    i = pl.program_id(0)
    n_chunks = steps // SSM_CHUNK
    rows_cb = n_chunks * batch
    pitch = steps + 8
    n_tiles = SSM_WIDTH // LANE_TILE

    @pl.when(i == 0)
    def _():
        st[...] = jnp.zeros_like(st)
        st_sw[...] = jnp.zeros_like(st_sw)

    x = x_ref[...].reshape(batch * steps, x_ref.shape[-1])
    h = _rms_norm(x, gain_ref[...]).astype(BF16)
    h_ref[...] = h.reshape(h_ref.shape)
    u_bt = jnp.dot(h, wu_ref[...], preferred_element_type=F32)

    for j in range(n_tiles):
        for b in range(batch):
            slab_in[j, pitch * b:pitch * b + steps, :] = (
                u_bt[steps * b:steps * (b + 1), LANE_TILE * j:LANE_TILE * (j + 1)])
    group_lanes = GROUPS_PER_TILE * STATE_LANES
    for j in range(n_tiles):
        lanes_j = slice(LANE_TILE * j, LANE_TILE * (j + 1))
        by_step = [
            jnp.concatenate([slab_in[j, pl.ds(SSM_CHUNK * c + tau, batch, stride=pitch), :]
                             for c in range(n_chunks)], axis=0)
            for tau in range(SSM_CHUNK)]
        for tau in range(SSM_CHUNK):
            ur_s[rows_cb * tau:rows_cb * (tau + 1), lanes_j] = by_step[tau]
        by_group = _block_transpose(by_step)

        intra = []
        for g8 in range(GROUPS_PER_TILE):
            g = GROUPS_PER_TILE * j + g8
            ug = by_group[g8].astype(BF16)
            intra.append(jnp.dot(ug, m_ref[g], preferred_element_type=F32))
            state = jnp.dot(ug, ws_ref[g], preferred_element_type=F32)
            sc_s[:, STATE_LANES * g:STATE_LANES * (g + 1)] = state[:, :STATE_LANES]
            scw_s[:, STATE_LANES * g:STATE_LANES * (g + 1)] = state[:, STATE_LANES:]

        sl = slice(group_lanes * j, group_lanes * (j + 1))
        aa = jnp.broadcast_to(aa_ref[:, sl], (batch, group_lanes))
        ab = jnp.broadcast_to(ab_ref[:, sl], (batch, group_lanes))
        hh, hw = st[:, sl], st_sw[:, sl]
        for c in range(n_chunks):
            rows = slice(batch * c, batch * (c + 1))
            hp_s[rows, sl] = hh
            hh, hw = (aa * hh + ab * hw + sc_s[rows, sl], aa * hw - ab * hh + scw_s[rows, sl])
        st[:, sl] = hh
        st_sw[:, sl] = hw

        by_group = []
        for g8 in range(GROUPS_PER_TILE):
            g = GROUPS_PER_TILE * j + g8
            hp = hp_s[:, STATE_LANES * g:STATE_LANES * (g + 1)].astype(BF16)
            by_group.append(intra[g8] + jnp.dot(hp, p_ref[g], preferred_element_type=F32))
        by_step = _block_transpose(by_group)
        for tau in range(SSM_CHUNK):
            rows = slice(rows_cb * tau, rows_cb * (tau + 1))
            ur_s[rows, lanes_j] = by_step[tau] + d_ref[:, lanes_j] * ur_s[rows, lanes_j]

    g_act = jax.nn.gelu(ur_s[...])
    z = jnp.dot(g_act.astype(BF16), wglu_ref[...], preferred_element_type=F32) + bglu_ref[...]
    s = z[:, :SSM_WIDTH] * jax.nn.sigmoid(z[:, SSM_WIDTH:])

    for tau in range(SSM_CHUNK):
        for c in range(n_chunks):
            r0 = rows_cb * tau + batch * c
            for j in range(n_tiles):
                slab_out[j, pl.ds(SSM_CHUNK * c + tau, batch, stride=pitch), :] = (
                    s[r0:r0 + batch, LANE_TILE * j:LANE_TILE * (j + 1)])
    for b in range(batch):
        s_ref[b] = jnp.concatenate(
            [slab_out[j, pitch * b:pitch * b + steps, :] for j in range(n_tiles)], axis=1)


def _ssm(x, w_all, consts, steps):
    batch, l, d = x.shape
    n_tiles = SSM_WIDTH // LANE_TILE
    rows_cb = steps // SSM_CHUNK * batch
    kern = functools.partial(_ssm_kernel, steps=steps, batch=batch)
    slab = pltpu.VMEM((n_tiles, (steps + 8) * batch, LANE_TILE), F32)
    state_cols = SSM_GROUPS * STATE_LANES
    return pl.pallas_call(
        kern,
        grid=(l // steps,),
        in_specs=[pl.BlockSpec((batch, steps, d), lambda i: (0, i, 0)), _const_spec(consts[0].shape),
                  _col_block_spec(d, SSM_WIDTH, U_COL_BLOCK)]
        + [_const_spec(c.shape) for c in consts[1:]],
        out_specs=[pl.BlockSpec((batch, steps, SSM_WIDTH), lambda i: (0, i, 0)),
                   pl.BlockSpec((batch, steps, d), lambda i: (0, i, 0))],
        out_shape=[jax.ShapeDtypeStruct((batch, l, SSM_WIDTH), F32),
                   jax.ShapeDtypeStruct((batch, l, d), BF16)],
        scratch_shapes=[
            slab, slab,
            pltpu.VMEM((steps * batch, SSM_WIDTH), F32),
            pltpu.VMEM((rows_cb, state_cols), F32), pltpu.VMEM((rows_cb, state_cols), F32),
            pltpu.VMEM((rows_cb, state_cols), F32),
            pltpu.VMEM((batch, state_cols), F32), pltpu.VMEM((batch, state_cols), F32),
        ],
        compiler_params=pltpu.CompilerParams(
            dimension_semantics=("arbitrary",), vmem_limit_bytes=VMEM_LIMIT_BYTES),
        name="ssm",
    )(x, consts[0], w_all, *consts[1:])


PAIR = 2 * CHUNK
WINDOW = BAND + CHUNK
N_PAIRS = N_HEADS // 2
U_COL_BLOCK = 4
MAIN_W_IN_BLOCKS = ((ATTN_WIDTH, 0), (ATTN_WIDTH, 1), (ATTN_WIDTH, 2), (ATTN_WIDTH, 3),
                    (SSM_WIDTH, 5), (2 * ATTN_WIDTH, 3), (2 * ATTN_WIDTH, 4))


class _AttnRefs:
    def __init__(self, bias_ref, qm_s, k_s, vt_s, yat_s, s_bufs, p_bufs, r_bufs):
        self.bias_ref, self.qm_s, self.k_s, self.vt_s, self.yat_s = bias_ref, qm_s, k_s, vt_s, yat_s
        self.s_bufs, self.p_bufs, self.r_bufs = s_bufs, p_bufs, r_bufs


def _unit_index(k):
    if isinstance(k, int):
        return k // N_PAIRS, k % N_PAIRS
    return lax.shift_right_logical(k, 2), lax.bitwise_and(k, N_PAIRS - 1)


def _row_start(idx, size):
    return idx * size if isinstance(idx, int) else pl.multiple_of(idx * size, size)


def _nt_dot(a, b):
    return lax.dot_general(a, b, (((1,), (1,)), ((), ())), preferred_element_type=F32)


def _scores_stage(r, k, slot, masked_start):
    m, pr = _unit_index(k)
    kwin = r.k_s[pr, pl.ds(_row_start(m, PAIR), WINDOW), :]
    s = _nt_dot(kwin, r.qm_s[pr, m]) + r.bias_ref[pr]
    if masked_start:
        row = lax.broadcasted_iota(jnp.int32, s.shape, 0)
        s = jnp.where(row >= HIST - PAIR * m, s, NEG_INF)
    r.s_bufs[slot][...] = s


def _softmax_stage(r, slot):
    s = r.s_bufs[slot][...]
    p = jnp.exp2(s - jnp.max(s, axis=0, keepdims=True))
    r.r_bufs[slot][...] = 1.0 / jnp.sum(p, axis=0, keepdims=True)
    r.p_bufs[slot][...] = p.astype(BF16)


def _values_stage(r, k, slot):
    m, pr = _unit_index(k)
    vwin = r.vt_s[pl.ds(_row_start(pr, PAIR), PAIR), pl.ds(_row_start(m, PAIR), WINDOW)]
    o = jnp.dot(vwin, r.p_bufs[slot][...], preferred_element_type=F32) * r.r_bufs[slot][...]
    r.yat_s[m, pl.ds(_row_start(pr, PAIR), PAIR), :] = jnp.concatenate(
        [o[:HEAD_DIM, :PAIR], o[HEAD_DIM:, PAIR:]], axis=0)


def _attention(r, n_units, masked_start):
    _scores_stage(r, 0, 0, masked_start)
    _scores_stage(r, 1, 1, masked_start)
    _softmax_stage(r, 0)

    def body(kk, carry):
        k = 2 * kk + 2
        _scores_stage(r, k, 0, masked_start)
        _softmax_stage(r, 1)
        _values_stage(r, k - 2, 0)
        _scores_stage(r, k + 1, 1, masked_start)
        _softmax_stage(r, 0)
        _values_stage(r, k - 1, 1)
        return carry

    lax.fori_loop(0, (n_units - 2) // 2, body, 0)
    _softmax_stage(r, 1)
    _values_stage(r, n_units - 2, 0)
    _values_stage(r, n_units - 1, 1)


def _main_kernel(x_ref, h_ref, s_ref, wq_ref, wk_ref, wv_ref, wza_ref, wzs_ref, wga_ref, wgs_ref,
                 gb_ref, band_bias_ref, wao_ref, wso_ref, wo_ref, fgain_ref, o_ref,
                 k_s, vt_s, qm_s, yat_s, s_buf0, s_buf1, p_buf0, p_buf1, r_buf0, r_buf1,
                 bias_s, *, tm):
    i = pl.program_id(1)
    n_m = tm // PAIR

    @pl.when((pl.program_id(0) == 0) & (i == 0))
    def _():
        masked = jnp.full((CHUNK, CHUNK), NEG_INF, F32)
        for pr in range(N_PAIRS):
            blocks = []
            for head in (2 * pr, 2 * pr + 1):
                band = band_bias_ref[head]
                blocks.append(jnp.concatenate([band, masked], axis=1))
                blocks.append(jnp.concatenate([masked, band], axis=1))
            bias_s[pr] = jnp.concatenate(blocks, axis=0).T

    h = h_ref[...]

    q = jnp.dot(h, wq_ref[...], preferred_element_type=F32)
    lane = lax.broadcasted_iota(jnp.int32, (tm, PAIR), 1)
    for pr in range(N_PAIRS):
        qp = q[:, PAIR * pr:PAIR * (pr + 1)]
        lo = jnp.where(lane < HEAD_DIM, qp, 0.0).astype(BF16)
        hi = jnp.where(lane >= HEAD_DIM, qp, 0.0).astype(BF16)
        for m in range(n_m):
            qm_s[pr, m, 0:PAIR, :] = lo[PAIR * m:PAIR * (m + 1)]
            qm_s[pr, m, PAIR:2 * PAIR, :] = hi[PAIR * m:PAIR * (m + 1)]

    k_new = jnp.dot(h, wk_ref[...], preferred_element_type=F32)
    vt_new = jnp.dot(h, wv_ref[...], preferred_element_type=F32).T

    @pl.when(i == 0)
    def _():
        k_s[:, 0:HIST, :] = jnp.zeros((N_PAIRS, HIST, PAIR), BF16)
        vt_s[:, 0:HIST] = jnp.zeros((ATTN_WIDTH, HIST), BF16)

    @pl.when(i > 0)
    def _():
        k_s[:, 0:HIST, :] = k_s[:, tm:tm + HIST, :]
        vt_s[:, 0:HIST] = vt_s[:, tm:tm + HIST]

    vt_s[:, HIST:HIST + tm] = vt_new.astype(BF16)
    for pr in range(N_PAIRS):
        k_s[pr, HIST:HIST + tm, :] = k_new[:, PAIR * pr:PAIR * (pr + 1)].astype(BF16)

    refs = _AttnRefs(bias_s, qm_s, k_s, vt_s, yat_s,
                     (s_buf0, s_buf1), (p_buf0, p_buf1), (r_buf0, r_buf1))

    @pl.when(i == 0)
    def _():
        _attention(refs, n_m * N_PAIRS, True)

    @pl.when(i > 0)
    def _():
        _attention(refs, n_m * N_PAIRS, False)

    za = jnp.dot(h, wza_ref[...], preferred_element_type=F32)
    yat = jnp.concatenate([yat_s[m] for m in range(n_m)], axis=1)
    ya = yat.T * jax.nn.silu(za)
    yao = jnp.dot(ya.astype(BF16), wao_ref[...], preferred_element_type=F32)
    zs = jnp.dot(h, wzs_ref[...], preferred_element_type=F32)
    ys = s_ref[...] * jax.nn.silu(zs)
    yso = jnp.dot(ys.astype(BF16), wso_ref[...], preferred_element_type=F32)
    d = x_ref.shape[-1]
    ga = jnp.dot(h, wga_ref[...], preferred_element_type=F32) + gb_ref[:, :d]
    gs = jnp.dot(h, wgs_ref[...], preferred_element_type=F32) + gb_ref[:, d:]
    merged = jax.nn.sigmoid(ga) * yao + jax.nn.sigmoid(gs) * yso
    xn = x_ref[...] + jnp.dot(merged.astype(BF16), wo_ref[...], preferred_element_type=F32)
    o_ref[...] = _rms_norm(xn, fgain_ref[...])


def _main(x, h, s, w_all, consts):
    b, l, d = x.shape
    tm = MAIN_TILE
    n_m = tm // PAIR
    kern = functools.partial(_main_kernel, tm=tm)
    unit_rows = 2 * PAIR
    return pl.pallas_call(
        kern,
        grid=(b, l // tm),
        in_specs=[
            pl.BlockSpec((None, tm, d), lambda bi, i: (bi, i, 0)),
            pl.BlockSpec((None, tm, d), lambda bi, i: (bi, i, 0)),
            pl.BlockSpec((None, tm, SSM_WIDTH), lambda bi, i: (bi, i, 0)),
        ] + [_col_block_spec(d, width, index) for width, index in MAIN_W_IN_BLOCKS]
        + [_const_spec(c.shape) for c in consts],
        out_specs=pl.BlockSpec((None, tm, d), lambda bi, i: (bi, i, 0)),
        out_shape=jax.ShapeDtypeStruct((b, l, d), F32),
        scratch_shapes=[
            pltpu.VMEM((N_PAIRS, HIST + tm, PAIR), BF16),
            pltpu.VMEM((ATTN_WIDTH, HIST + tm), BF16),
            pltpu.VMEM((N_PAIRS, n_m, unit_rows, PAIR), BF16),
            pltpu.VMEM((n_m, ATTN_WIDTH, PAIR), F32),
            pltpu.VMEM((WINDOW, unit_rows), F32), pltpu.VMEM((WINDOW, unit_rows), F32),
            pltpu.VMEM((WINDOW, unit_rows), BF16), pltpu.VMEM((WINDOW, unit_rows), BF16),
            pltpu.VMEM((1, unit_rows), F32), pltpu.VMEM((1, unit_rows), F32),
            pltpu.VMEM((N_PAIRS, WINDOW, unit_rows), F32),
        ],
        compiler_params=pltpu.CompilerParams(
            dimension_semantics=("arbitrary", "arbitrary"), vmem_limit_bytes=VMEM_LIMIT_BYTES),
        name="main_block",
    )(x, h, s, *([w_all] * len(MAIN_W_IN_BLOCKS)), *consts)


def _ssm_chunk_params(a_re, a_im, log_dt, b_re, b_im, c_re, c_im):
    hi = lax.Precision.HIGHEST
    n = SSM_CHUNK
    dt = jnp.exp(log_dt)[:, None]
    k = jnp.arange(n + 1, dtype=F32)[None, :, None]
    mag = jnp.exp((a_re * dt)[:, None, :] * k)
    pw_re = mag * jnp.cos((a_im * dt)[:, None, :] * k)
    pw_im = mag * jnp.sin((a_im * dt)[:, None, :] * k)
    n_re = pw_re[:, 1] - 1.0
    n_im = pw_im[:, 1]
    den = a_re * a_re + a_im * a_im
    f_re = ((n_re * a_re + n_im * a_im) / den)[:, None, :]
    f_im = ((n_im * a_re - n_re * a_im) / den)[:, None, :]
    bt_re, bt_im = jnp.swapaxes(b_re, 1, 2), jnp.swapaxes(b_im, 1, 2)
    bb_re = f_re * bt_re - f_im * bt_im
    bb_im = f_re * bt_im + f_im * bt_re
    w_re = pw_re[:, :n, None, :] * bb_re[:, None] - pw_im[:, :n, None, :] * bb_im[:, None]
    w_im = pw_re[:, :n, None, :] * bb_im[:, None] + pw_im[:, :n, None, :] * bb_re[:, None]
    kern = (jnp.einsum("gop,gkcp->gcko", c_re, w_re, precision=hi)
            - jnp.einsum("gop,gkcp->gcko", c_im, w_im, precision=hi))
    kern = kern.reshape(SSM_GROUPS, SSM_GROUP, LANE_TILE)
    zeros = jnp.zeros_like(kern)
    m = jnp.stack([jnp.concatenate([zeros[:, :, :SSM_GROUP * sig], kern[:, :, :SSM_GROUP * (n - sig)]],
                                   axis=2) for sig in range(n)], axis=1)
    m = m.reshape(SSM_GROUPS, LANE_TILE, LANE_TILE).astype(BF16)
    ws_re = w_re[:, ::-1].reshape(SSM_GROUPS, LANE_TILE, SSM_STATE)
    ws_im = w_im[:, ::-1].reshape(SSM_GROUPS, LANE_TILE, SSM_STATE)
    ws = jnp.concatenate([ws_re, ws_im, ws_im, ws_re], axis=2).astype(BF16)
    ct_re, ct_im = jnp.swapaxes(c_re, 1, 2)[:, :, None, :], jnp.swapaxes(c_im, 1, 2)[:, :, None, :]
    pt_re = jnp.swapaxes(pw_re[:, 1:], 1, 2)[..., None]
    pt_im = jnp.swapaxes(pw_im[:, 1:], 1, 2)[..., None]
    p_re = (ct_re * pt_re - ct_im * pt_im).reshape(SSM_GROUPS, SSM_STATE, LANE_TILE)
    p_im = (ct_re * pt_im + ct_im * pt_re).reshape(SSM_GROUPS, SSM_STATE, LANE_TILE)
    p = jnp.concatenate([p_re, -p_im], axis=1).astype(BF16)
    aa = jnp.concatenate([pw_re[:, n], pw_re[:, n]], axis=1).reshape(1, -1)
    ab = jnp.concatenate([-pw_im[:, n], pw_im[:, n]], axis=1).reshape(1, -1)
    return m, ws, p, aa, ab


def _band_bias(rel_bias):
    n_rel = rel_bias.shape[1]
    ext = jnp.concatenate(
        [rel_bias, jnp.broadcast_to(rel_bias[:, -1:], (N_HEADS, CHUNK + BAND - 1 - n_rel))], axis=1)
    by_rev_k = jnp.stack([ext[:, q:q + BAND] for q in range(CHUNK)], axis=1)
    return by_rev_k[:, :, ::-1] * LOG2E


def kernel(x, norm_gain, w_in, rel_bias, ssm_a_re, ssm_a_im, ssm_log_dt, ssm_b_re, ssm_b_im,
           ssm_c_re, ssm_c_im, ssm_d, w_glu, b_glu, w_attn_out, w_ssm_out, gate_bias, w_out,
           final_gain):
    assert norm_gain.shape[0] == 1, "single-layer block"
    b, l, d = x.shape
    aw, sw = ATTN_WIDTH, SSM_WIDTH
    col_scale = jnp.where(jnp.arange(w_in.shape[-1]) < aw, LOG2E / math.sqrt(HEAD_DIM), 1.0)
    w_all = (w_in[0] * col_scale).astype(BF16)
    gain = norm_gain[0].reshape(1, d)

    m, ws, p, aa, ab = _ssm_chunk_params(
        ssm_a_re[0], ssm_a_im[0], ssm_log_dt[0], ssm_b_re[0], ssm_b_im[0], ssm_c_re[0], ssm_c_im[0])
    s, h = _ssm(x, w_all, [gain, m, ws, p, aa, ab, ssm_d[0].reshape(1, sw),
                        w_glu[0].astype(BF16), b_glu[0].reshape(1, 2 * sw)], steps=SSM_STEPS)

    consts = [gate_bias[0].reshape(1, 2 * d), _band_bias(rel_bias[0]),
              w_attn_out[0].astype(BF16), w_ssm_out[0].astype(BF16), w_out[0].astype(BF16),
              final_gain.reshape(1, d)]
    return _main(x, h, s, w_all, consts)
```

```python
import functools
import math

import jax
import jax.numpy as jnp
from jax import lax
from jax.experimental import pallas as pl
from jax.experimental.pallas import tpu as pltpu

F32 = jnp.float32
BF16 = jnp.bfloat16

CHUNK = 64
LEFT_CHUNKS = 8
BAND = (LEFT_CHUNKS + 1) * CHUNK
HIST = LEFT_CHUNKS * CHUNK
N_HEADS = 8
HEAD_DIM = 64
ATTN_WIDTH = N_HEADS * HEAD_DIM
SSM_WIDTH = 512
SSM_GROUP = 16
SSM_GROUPS = SSM_WIDTH // SSM_GROUP
SSM_STATE = 64
NORM_EPS = 1e-6
NEG_INF = -1e30
LOG2E = math.log2(math.e)

V7X_VMEM_BYTES = 64 * 1024 * 1024
VMEM_LIMIT_BYTES = V7X_VMEM_BYTES - 8 * 1024 * 1024
SSM_STEPS = 128
MAIN_TILE = HIST


def _rms_norm(x, gain):
    inv = lax.rsqrt(jnp.mean(x * x, axis=-1, keepdims=True) + NORM_EPS)
    return x * inv * gain


def _const_spec(shape):
    zeros = (0,) * len(shape)
    return pl.BlockSpec(shape, lambda *_: zeros, pipeline_mode=pl.Buffered(1))


def _col_block_spec(rows, width, index):
    return pl.BlockSpec((rows, width), lambda *_: (0, index), pipeline_mode=pl.Buffered(1))


LANE_TILE = 128
SSM_CHUNK = LANE_TILE // SSM_GROUP
GROUPS_PER_TILE = LANE_TILE // SSM_GROUP
STATE_LANES = 2 * SSM_STATE


def _block_transpose(arrs):
    arrs = list(arrs)
    lane_block = lax.broadcasted_iota(jnp.int32, arrs[0].shape, 1) // SSM_GROUP
    for d in (4, 2, 1):
        low = (lane_block & d) == 0
        for lo in range(len(arrs)):
            if lo & d:
                continue
            a, b = arrs[lo], arrs[lo + d]
            arrs[lo] = jnp.where(low, a, pltpu.roll(b, SSM_GROUP * d, axis=1))
            arrs[lo + d] = jnp.where(low, pltpu.roll(a, LANE_TILE - SSM_GROUP * d, axis=1), b)
    return arrs


def _ssm_kernel(x_ref, gain_ref, wu_ref, m_ref, ws_ref, p_ref, aa_ref, ab_ref, d_ref, wglu_ref, bglu_ref,
                s_ref, h_ref, slab_in, slab_out, ur_s, sc_s, scw_s, hp_s, st, st_sw,
                *, steps, batch):
    i = pl.program_id(0)
    n_chunks = steps // SSM_CHUNK
    rows_cb = n_chunks * batch
    pitch = steps + 8
    n_tiles = SSM_WIDTH // LANE_TILE

    @pl.when(i == 0)
    def _():
        st[...] = jnp.zeros_like(st)
        st_sw[...] = jnp.zeros_like(st_sw)

    x = x_ref[...].reshape(batch * steps, x_ref.shape[-1])
    h = _rms_norm(x, gain_ref[...]).astype(BF16)
    h_ref[...] = h.reshape(h_ref.shape)
    u_bt = jnp.dot(h, wu_ref[...], preferred_element_type=F32)

    for j in range(n_tiles):
        for b in range(batch):
            slab_in[j, pitch * b:pitch * b + steps, :] = (
                u_bt[steps * b:steps * (b + 1), LANE_TILE * j:LANE_TILE * (j + 1)])
    group_lanes = GROUPS_PER_TILE * STATE_LANES
    for j in range(n_tiles):
        lanes_j = slice(LANE_TILE * j, LANE_TILE * (j + 1))
        by_step = [
            jnp.concatenate([slab_in[j, pl.ds(SSM_CHUNK * c + tau, batch, stride=pitch), :]
                             for c in range(n_chunks)], axis=0)
            for tau in range(SSM_CHUNK)]
        for tau in range(SSM_CHUNK):
            ur_s[rows_cb * tau:rows_cb * (tau + 1), lanes_j] = by_step[tau]
        by_group = _block_transpose(by_step)

        intra = []
        for g8 in range(GROUPS_PER_TILE):
            g = GROUPS_PER_TILE * j + g8
            ug = by_group[g8].astype(BF16)
            intra.append(jnp.dot(ug, m_ref[g], preferred_element_type=F32))
            state = jnp.dot(ug, ws_ref[g], preferred_element_type=F32)
            sc_s[:, STATE_LANES * g:STATE_LANES * (g + 1)] = state[:, :STATE_LANES]
            scw_s[:, STATE_LANES * g:STATE_LANES * (g + 1)] = state[:, STATE_LANES:]

        sl = slice(group_lanes * j, group_lanes * (j + 1))
        aa = jnp.broadcast_to(aa_ref[:, sl], (batch, group_lanes))
        ab = jnp.broadcast_to(ab_ref[:, sl], (batch, group_lanes))
        hh, hw = st[:, sl], st_sw[:, sl]
        for c in range(n_chunks):
            rows = slice(batch * c, batch * (c + 1))
            hp_s[rows, sl] = hh
            hh, hw = (aa * hh + ab * hw + sc_s[rows, sl], aa * hw - ab * hh + scw_s[rows, sl])
        st[:, sl] = hh
        st_sw[:, sl] = hw

        by_group = []
        for g8 in range(GROUPS_PER_TILE):
            g = GROUPS_PER_TILE * j + g8
            hp = hp_s[:, STATE_LANES * g:STATE_LANES * (g + 1)].astype(BF16)
            by_group.append(intra[g8] + jnp.dot(hp, p_ref[g], preferred_element_type=F32))
        by_step = _block_transpose(by_group)
        for tau in range(SSM_CHUNK):
            rows = slice(rows_cb * tau, rows_cb * (tau + 1))
            ur_s[rows, lanes_j] = by_step[tau] + d_ref[:, lanes_j] * ur_s[rows, lanes_j]

    g_act = jax.nn.gelu(ur_s[...])
    z = jnp.dot(g_act.astype(BF16), wglu_ref[...], preferred_element_type=F32) + bglu_ref[...]
    s = z[:, :SSM_WIDTH] * jax.nn.sigmoid(z[:, SSM_WIDTH:])

    for tau in range(SSM_CHUNK):
        for c in range(n_chunks):
            r0 = rows_cb * tau + batch * c
            for j in range(n_tiles):
                slab_out[j, pl.ds(SSM_CHUNK * c + tau, batch, stride=pitch), :] = (
                    s[r0:r0 + batch, LANE_TILE * j:LANE_TILE * (j + 1)])
    for b in range(batch):
        s_ref[b] = jnp.concatenate(
            [slab_out[j, pitch * b:pitch * b + steps, :] for j in range(n_tiles)], axis=1)


def _ssm(x, w_all, consts, steps):
    batch, l, d = x.shape
    n_tiles = SSM_WIDTH // LANE_TILE
    rows_cb = steps // SSM_CHUNK * batch
    kern = functools.partial(_ssm_kernel, steps=steps, batch=batch)
    slab = pltpu.VMEM((n_tiles, (steps + 8) * batch, LANE_TILE), F32)
    state_cols = SSM_GROUPS * STATE_LANES
    return pl.pallas_call(
        kern,
        grid=(l // steps,),
        in_specs=[pl.BlockSpec((batch, steps, d), lambda i: (0, i, 0)), _const_spec(consts[0].shape),
                  _col_block_spec(d, SSM_WIDTH, U_COL_BLOCK)]
        + [_const_spec(c.shape) for c in consts[1:]],
        out_specs=[pl.BlockSpec((batch, steps, SSM_WIDTH), lambda i: (0, i, 0)),
                   pl.BlockSpec((batch, steps, d), lambda i: (0, i, 0))],
        out_shape=[jax.ShapeDtypeStruct((batch, l, SSM_WIDTH), F32),
                   jax.ShapeDtypeStruct((batch, l, d), BF16)],
        scratch_shapes=[
            slab, slab,
            pltpu.VMEM((steps * batch, SSM_WIDTH), F32),
            pltpu.VMEM((rows_cb, state_cols), F32), pltpu.VMEM((rows_cb, state_cols), F32),
            pltpu.VMEM((rows_cb, state_cols), F32),
            pltpu.VMEM((batch, state_cols), F32), pltpu.VMEM((batch, state_cols), F32),
        ],
        compiler_params=pltpu.CompilerParams(
            dimension_semantics=("arbitrary",), vmem_limit_bytes=VMEM_LIMIT_BYTES),
        name="ssm",
    )(x, consts[0], w_all, *consts[1:])


PAIR = 2 * CHUNK
WINDOW = BAND + CHUNK
N_PAIRS = N_HEADS // 2
U_COL_BLOCK = 4
MAIN_W_IN_BLOCKS = ((ATTN_WIDTH, 0), (ATTN_WIDTH, 1), (ATTN_WIDTH, 2), (ATTN_WIDTH, 3),
                    (SSM_WIDTH, 5), (2 * ATTN_WIDTH, 3), (2 * ATTN_WIDTH, 4))


class _AttnRefs:
    def __init__(self, bias_ref, qm_s, k_s, vt_s, yat_s, s_bufs, p_bufs, r_bufs):
        self.bias_ref, self.qm_s, self.k_s, self.vt_s, self.yat_s = bias_ref, qm_s, k_s, vt_s, yat_s
        self.s_bufs, self.p_bufs, self.r_bufs = s_bufs, p_bufs, r_bufs


def _unit_index(k):
    if isinstance(k, int):
        return k // N_PAIRS, k % N_PAIRS
    return lax.shift_right_logical(k, 2), lax.bitwise_and(k, N_PAIRS - 1)


def _row_start(idx, size):
    return idx * size if isinstance(idx, int) else pl.multiple_of(idx * size, size)


def _nt_dot(a, b):
    return lax.dot_general(a, b, (((1,), (1,)), ((), ())), preferred_element_type=F32)


def _scores_stage(r, k, slot, masked_start):
    m, pr = _unit_index(k)
    kwin = r.k_s[pr, pl.ds(_row_start(m, PAIR), WINDOW), :]
    s = _nt_dot(kwin, r.qm_s[pr, m]) + r.bias_ref[pr]
    row = lax.broadcasted_iota(jnp.int32, s.shape, 0)
    s = jnp.where(row >= masked_start * (HIST - PAIR * m), s, NEG_INF)
    r.s_bufs[slot][...] = s


def _softmax_stage(r, slot):
    s = r.s_bufs[slot][...]
    p = jnp.exp2(s - jnp.max(s, axis=0, keepdims=True))
    r.r_bufs[slot][...] = 1.0 / jnp.sum(p, axis=0, keepdims=True)
    r.p_bufs[slot][...] = p.astype(BF16)


def _values_stage(r, k, slot):
    m, pr = _unit_index(k)
    vwin = r.vt_s[pl.ds(_row_start(pr, PAIR), PAIR), pl.ds(_row_start(m, PAIR), WINDOW)]
    o = jnp.dot(vwin, r.p_bufs[slot][...], preferred_element_type=F32) * r.r_bufs[slot][...]
    r.yat_s[m, pl.ds(_row_start(pr, PAIR), PAIR), :] = jnp.concatenate(
        [o[:HEAD_DIM, :PAIR], o[HEAD_DIM:, PAIR:]], axis=0)


def _attention(r, n_units, masked_start):
    _scores_stage(r, 0, 0, masked_start)
    _scores_stage(r, 1, 1, masked_start)
    _softmax_stage(r, 0)

    def body(kk, carry):
        k = 2 * kk + 2
        _scores_stage(r, k, 0, masked_start)
        _softmax_stage(r, 1)
        _values_stage(r, k - 2, 0)
        _scores_stage(r, k + 1, 1, masked_start)
        _softmax_stage(r, 0)
        _values_stage(r, k - 1, 1)
        return carry

    lax.fori_loop(0, (n_units - 2) // 2, body, 0, unroll=True)
    _softmax_stage(r, 1)
    _values_stage(r, n_units - 2, 0)
    _values_stage(r, n_units - 1, 1)


def _main_kernel(x_ref, h_ref, s_ref, wq_ref, wk_ref, wv_ref, wza_ref, wzs_ref, wga_ref, wgs_ref,
                 gb_ref, band_bias_ref, wao_ref, wso_ref, wo_ref, fgain_ref, o_ref,
                 k_s, vt_s, qm_s, yat_s, s_buf0, s_buf1, p_buf0, p_buf1, r_buf0, r_buf1,
                 bias_s, *, tm):
    i = pl.program_id(1)
    n_m = tm // PAIR

    @pl.when((pl.program_id(0) == 0) & (i == 0))
    def _():
        masked = jnp.full((CHUNK, CHUNK), NEG_INF, F32)
        for pr in range(N_PAIRS):
            blocks = []
            for head in (2 * pr, 2 * pr + 1):
                band = band_bias_ref[head]
                blocks.append(jnp.concatenate([band, masked], axis=1))
                blocks.append(jnp.concatenate([masked, band], axis=1))
            bias_s[pr] = jnp.concatenate(blocks, axis=0).T
        k_s[:, 0:HIST, :] = jnp.zeros((N_PAIRS, HIST, PAIR), BF16)
        vt_s[:, 0:HIST] = jnp.zeros((ATTN_WIDTH, HIST), BF16)

    h = h_ref[...]

    q = jnp.dot(h, wq_ref[...], preferred_element_type=F32)
    lane = lax.broadcasted_iota(jnp.int32, (tm, PAIR), 1)
    for pr in range(N_PAIRS):
        qp = q[:, PAIR * pr:PAIR * (pr + 1)]
        lo = jnp.where(lane < HEAD_DIM, qp, 0.0).astype(BF16)
        hi = jnp.where(lane >= HEAD_DIM, qp, 0.0).astype(BF16)
        for m in range(n_m):
            qm_s[pr, m, 0:PAIR, :] = lo[PAIR * m:PAIR * (m + 1)]
            qm_s[pr, m, PAIR:2 * PAIR, :] = hi[PAIR * m:PAIR * (m + 1)]

    k_new = jnp.dot(h, wk_ref[...], preferred_element_type=F32)
    vt_new = jnp.dot(h, wv_ref[...], preferred_element_type=F32).T

    vt_s[:, HIST:HIST + tm] = vt_new.astype(BF16)
    for pr in range(N_PAIRS):
        k_s[pr, HIST:HIST + tm, :] = k_new[:, PAIR * pr:PAIR * (pr + 1)].astype(BF16)

    refs = _AttnRefs(bias_s, qm_s, k_s, vt_s, yat_s,
                     (s_buf0, s_buf1), (p_buf0, p_buf1), (r_buf0, r_buf1))

    _attention(refs, n_m * N_PAIRS, (i == 0).astype(jnp.int32))

    za = jnp.dot(h, wza_ref[...], preferred_element_type=F32)
    yat = jnp.concatenate([yat_s[m] for m in range(n_m)], axis=1)
    ya = yat.T * jax.nn.silu(za)
    yao = jnp.dot(ya.astype(BF16), wao_ref[...], preferred_element_type=F32)
    zs = jnp.dot(h, wzs_ref[...], preferred_element_type=F32)
    ys = s_ref[...] * jax.nn.silu(zs)
    yso = jnp.dot(ys.astype(BF16), wso_ref[...], preferred_element_type=F32)
    d = x_ref.shape[-1]
    ga = jnp.dot(h, wga_ref[...], preferred_element_type=F32) + gb_ref[:, :d]
    gs = jnp.dot(h, wgs_ref[...], preferred_element_type=F32) + gb_ref[:, d:]
    merged = jax.nn.sigmoid(ga) * yao + jax.nn.sigmoid(gs) * yso
    xn = x_ref[...] + jnp.dot(merged.astype(BF16), wo_ref[...], preferred_element_type=F32)
    o_ref[...] = _rms_norm(xn, fgain_ref[...])
    k_s[:, 0:HIST, :] = k_s[:, tm:tm + HIST, :]
    vt_s[:, 0:HIST] = vt_s[:, tm:tm + HIST]


def _main(x, h, s, w_all, consts):
    b, l, d = x.shape
    tm = MAIN_TILE
    n_m = tm // PAIR
    kern = functools.partial(_main_kernel, tm=tm)
    unit_rows = 2 * PAIR
    return pl.pallas_call(
        kern,
        grid=(b, l // tm),
        in_specs=[
            pl.BlockSpec((None, tm, d), lambda bi, i: (bi, i, 0)),
            pl.BlockSpec((None, tm, d), lambda bi, i: (bi, i, 0)),
            pl.BlockSpec((None, tm, SSM_WIDTH), lambda bi, i: (bi, i, 0)),
        ] + [_col_block_spec(d, width, index) for width, index in MAIN_W_IN_BLOCKS]
        + [_const_spec(c.shape) for c in consts],
        out_specs=pl.BlockSpec((None, tm, d), lambda bi, i: (bi, i, 0)),
        out_shape=jax.ShapeDtypeStruct((b, l, d), F32),
        scratch_shapes=[
            pltpu.VMEM((N_PAIRS, HIST + tm, PAIR), BF16),
            pltpu.VMEM((ATTN_WIDTH, HIST + tm), BF16),
            pltpu.VMEM((N_PAIRS, n_m, unit_rows, PAIR), BF16),
            pltpu.VMEM((n_m, ATTN_WIDTH, PAIR), F32),
            pltpu.VMEM((WINDOW, unit_rows), F32), pltpu.VMEM((WINDOW, unit_rows), F32),
            pltpu.VMEM((WINDOW, unit_rows), BF16), pltpu.VMEM((WINDOW, unit_rows), BF16),
            pltpu.VMEM((1, unit_rows), F32), pltpu.VMEM((1, unit_rows), F32),
            pltpu.VMEM((N_PAIRS, WINDOW, unit_rows), F32),
        ],
        compiler_params=pltpu.CompilerParams(
            dimension_semantics=("arbitrary", "arbitrary"), vmem_limit_bytes=VMEM_LIMIT_BYTES),
        name="main_block",
    )(x, h, s, *([w_all] * len(MAIN_W_IN_BLOCKS)), *consts)


def _ssm_chunk_params(a_re, a_im, log_dt, b_re, b_im, c_re, c_im):
    hi = lax.Precision.HIGHEST
    n = SSM_CHUNK
    dt = jnp.exp(log_dt)[:, None]
    k = jnp.arange(n + 1, dtype=F32)[None, :, None]
    mag = jnp.exp((a_re * dt)[:, None, :] * k)
    pw_re = mag * jnp.cos((a_im * dt)[:, None, :] * k)
    pw_im = mag * jnp.sin((a_im * dt)[:, None, :] * k)
    n_re = pw_re[:, 1] - 1.0
    n_im = pw_im[:, 1]
    den = a_re * a_re + a_im * a_im
    f_re = ((n_re * a_re + n_im * a_im) / den)[:, None, :]
    f_im = ((n_im * a_re - n_re * a_im) / den)[:, None, :]
    bt_re, bt_im = jnp.swapaxes(b_re, 1, 2), jnp.swapaxes(b_im, 1, 2)
    bb_re = f_re * bt_re - f_im * bt_im
    bb_im = f_re * bt_im + f_im * bt_re
    w_re = pw_re[:, :n, None, :] * bb_re[:, None] - pw_im[:, :n, None, :] * bb_im[:, None]
    w_im = pw_re[:, :n, None, :] * bb_im[:, None] + pw_im[:, :n, None, :] * bb_re[:, None]
    kern = (jnp.einsum("gop,gkcp->gcko", c_re, w_re, precision=hi)
            - jnp.einsum("gop,gkcp->gcko", c_im, w_im, precision=hi))
    kern = kern.reshape(SSM_GROUPS, SSM_GROUP, LANE_TILE)
    zeros = jnp.zeros_like(kern)
    m = jnp.stack([jnp.concatenate([zeros[:, :, :SSM_GROUP * sig], kern[:, :, :SSM_GROUP * (n - sig)]],
                                   axis=2) for sig in range(n)], axis=1)
    m = m.reshape(SSM_GROUPS, LANE_TILE, LANE_TILE).astype(BF16)
    ws_re = w_re[:, ::-1].reshape(SSM_GROUPS, LANE_TILE, SSM_STATE)
    ws_im = w_im[:, ::-1].reshape(SSM_GROUPS, LANE_TILE, SSM_STATE)
    ws = jnp.concatenate([ws_re, ws_im, ws_im, ws_re], axis=2).astype(BF16)
    ct_re, ct_im = jnp.swapaxes(c_re, 1, 2)[:, :, None, :], jnp.swapaxes(c_im, 1, 2)[:, :, None, :]
    pt_re = jnp.swapaxes(pw_re[:, 1:], 1, 2)[..., None]
    pt_im = jnp.swapaxes(pw_im[:, 1:], 1, 2)[..., None]
    p_re = (ct_re * pt_re - ct_im * pt_im).reshape(SSM_GROUPS, SSM_STATE, LANE_TILE)
    p_im = (ct_re * pt_im + ct_im * pt_re).reshape(SSM_GROUPS, SSM_STATE, LANE_TILE)
    p = jnp.concatenate([p_re, -p_im], axis=1).astype(BF16)
    aa = jnp.concatenate([pw_re[:, n], pw_re[:, n]], axis=1).reshape(1, -1)
    ab = jnp.concatenate([-pw_im[:, n], pw_im[:, n]], axis=1).reshape(1, -1)
    return m, ws, p, aa, ab


def _band_bias(rel_bias):
    n_rel = rel_bias.shape[1]
    ext = jnp.concatenate(
        [rel_bias, jnp.broadcast_to(rel_bias[:, -1:], (N_HEADS, CHUNK + BAND - 1 - n_rel))], axis=1)
    by_rev_k = jnp.stack([ext[:, q:q + BAND] for q in range(CHUNK)], axis=1)
    return by_rev_k[:, :, ::-1] * LOG2E


def kernel(x, norm_gain, w_in, rel_bias, ssm_a_re, ssm_a_im, ssm_log_dt, ssm_b_re, ssm_b_im,
           ssm_c_re, ssm_c_im, ssm_d, w_glu, b_glu, w_attn_out, w_ssm_out, gate_bias, w_out,
           final_gain):
    assert norm_gain.shape[0] == 1, "single-layer block"
    b, l, d = x.shape
    aw, sw = ATTN_WIDTH, SSM_WIDTH
    col_scale = jnp.where(jnp.arange(w_in.shape[-1]) < aw, LOG2E / math.sqrt(HEAD_DIM), 1.0)
    w_all = (w_in[0] * col_scale).astype(BF16)
    gain = norm_gain[0].reshape(1, d)

    m, ws, p, aa, ab = _ssm_chunk_params(
        ssm_a_re[0], ssm_a_im[0], ssm_log_dt[0], ssm_b_re[0], ssm_b_im[0], ssm_c_re[0], ssm_c_im[0])
    s, h = _ssm(x, w_all, [gain, m, ws, p, aa, ab, ssm_d[0].reshape(1, sw),
                        w_glu[0].astype(BF16), b_glu[0].reshape(1, 2 * sw)], steps=SSM_STEPS)

    consts = [gate_bias[0].reshape(1, 2 * d), _band_bias(rel_bias[0]),
              w_attn_out[0].astype(BF16), w_ssm_out[0].astype(BF16), w_out[0].astype(BF16),
              final_gain.reshape(1, d)]
    return _main(x, h, s, w_all, consts)
```

```python
import functools
import math

import jax
import jax.numpy as jnp
from jax import lax
from jax.experimental import pallas as pl
from jax.experimental.pallas import tpu as pltpu

F32 = jnp.float32
BF16 = jnp.bfloat16

CHUNK = 64
LEFT_CHUNKS = 8
BAND = (LEFT_CHUNKS + 1) * CHUNK
HIST = LEFT_CHUNKS * CHUNK
N_HEADS = 8
HEAD_DIM = 64
ATTN_WIDTH = N_HEADS * HEAD_DIM
SSM_WIDTH = 512
SSM_GROUP = 16
SSM_GROUPS = SSM_WIDTH // SSM_GROUP
SSM_STATE = 64
NORM_EPS = 1e-6
NEG_INF = -1e30
LOG2E = math.log2(math.e)

V7X_VMEM_BYTES = 64 * 1024 * 1024
VMEM_LIMIT_BYTES = V7X_VMEM_BYTES - 8 * 1024 * 1024
SSM_STEPS = 128
MAIN_TILE = HIST


def _rms_norm(x, gain):
    inv = lax.rsqrt(jnp.mean(x * x, axis=-1, keepdims=True) + NORM_EPS)
    return x * inv * gain


def _const_spec(shape):
    zeros = (0,) * len(shape)
    return pl.BlockSpec(shape, lambda *_: zeros, pipeline_mode=pl.Buffered(1))


def _col_block_spec(rows, width, index):
    return pl.BlockSpec((rows, width), lambda *_: (0, index), pipeline_mode=pl.Buffered(1))


LANE_TILE = 128
SSM_CHUNK = LANE_TILE // SSM_GROUP
GROUPS_PER_TILE = LANE_TILE // SSM_GROUP
STATE_LANES = 2 * SSM_STATE


def _block_transpose(arrs):
    n = len(arrs)
    lane_block = lax.broadcasted_iota(jnp.int32, arrs[0].shape, 1) // SSM_GROUP
    rot = [a if i == 0 else pltpu.roll(a, SSM_GROUP * i, axis=1) for i, a in enumerate(arrs)]
    x = [rot[-k % n] for k in range(n)]
    for d in (1, 2, 4):
        has_bit = (lane_block & d) != 0
        x = [jnp.where(has_bit, x[(k - d) % n], x[k]) for k in range(n)]
    return [a if k == 0 else pltpu.roll(a, LANE_TILE - SSM_GROUP * k, axis=1) for k, a in enumerate(x)]


def _ssm_kernel(x_ref, gain_ref, wu_ref, m_ref, ws_ref, p_ref, aa_ref, ab_ref, d_ref, wglu_ref, bglu_ref,
                s_ref, h_ref, slab_in, slab_out, ur_s, sc_s, scw_s, hp_s, st, st_sw,
                *, steps, batch):
    i = pl.program_id(0)
    n_chunks = steps // SSM_CHUNK
    rows_cb = n_chunks * batch
    pitch = steps + 8
    n_tiles = SSM_WIDTH // LANE_TILE

    @pl.when(i == 0)
    def _():
        st[...] = jnp.zeros_like(st)
        st_sw[...] = jnp.zeros_like(st_sw)

    x = x_ref[...].reshape(batch * steps, x_ref.shape[-1])
    h = _rms_norm(x, gain_ref[...]).astype(BF16)
    h_ref[...] = h.reshape(h_ref.shape)
    u_bt = jnp.dot(h, wu_ref[...], preferred_element_type=F32)

    for j in range(n_tiles):
        for b in range(batch):
            slab_in[j, pitch * b:pitch * b + steps, :] = (
                u_bt[steps * b:steps * (b + 1), LANE_TILE * j:LANE_TILE * (j + 1)])
    group_lanes = GROUPS_PER_TILE * STATE_LANES
    for j in range(n_tiles):
        lanes_j = slice(LANE_TILE * j, LANE_TILE * (j + 1))
        by_step = [
            jnp.concatenate([slab_in[j, pl.ds(SSM_CHUNK * c + tau, batch, stride=pitch), :]
                             for c in range(n_chunks)], axis=0)
            for tau in range(SSM_CHUNK)]
        for tau in range(SSM_CHUNK):
            ur_s[rows_cb * tau:rows_cb * (tau + 1), lanes_j] = by_step[tau]
        by_group = _block_transpose(by_step)

        intra = []
        for g8 in range(GROUPS_PER_TILE):
            g = GROUPS_PER_TILE * j + g8
            ug = by_group[g8].astype(BF16)
            intra.append(jnp.dot(ug, m_ref[g], preferred_element_type=F32))
            state = jnp.dot(ug, ws_ref[g], preferred_element_type=F32)
            sc_s[:, STATE_LANES * g:STATE_LANES * (g + 1)] = state[:, :STATE_LANES]
            scw_s[:, STATE_LANES * g:STATE_LANES * (g + 1)] = state[:, STATE_LANES:]

        sl = slice(group_lanes * j, group_lanes * (j + 1))
        aa = jnp.broadcast_to(aa_ref[:, sl], (batch, group_lanes))
        ab = jnp.broadcast_to(ab_ref[:, sl], (batch, group_lanes))
        hh, hw = st[:, sl], st_sw[:, sl]
        for c in range(n_chunks):
            rows = slice(batch * c, batch * (c + 1))
            hp_s[rows, sl] = hh
            hh, hw = (aa * hh + ab * hw + sc_s[rows, sl], aa * hw - ab * hh + scw_s[rows, sl])
        st[:, sl] = hh
        st_sw[:, sl] = hw

        by_group = []
        for g8 in range(GROUPS_PER_TILE):
            g = GROUPS_PER_TILE * j + g8
            hp = hp_s[:, STATE_LANES * g:STATE_LANES * (g + 1)].astype(BF16)
            by_group.append(intra[g8] + jnp.dot(hp, p_ref[g], preferred_element_type=F32))
        by_step = _block_transpose(by_group)
        for tau in range(SSM_CHUNK):
            rows = slice(rows_cb * tau, rows_cb * (tau + 1))
            ur_s[rows, lanes_j] = by_step[tau] + d_ref[:, lanes_j] * ur_s[rows, lanes_j]

    g_act = jax.nn.gelu(ur_s[...])
    z = jnp.dot(g_act.astype(BF16), wglu_ref[...], preferred_element_type=F32) + bglu_ref[...]
    s = z[:, :SSM_WIDTH] * jax.nn.sigmoid(z[:, SSM_WIDTH:])

    for tau in range(SSM_CHUNK):
        for c in range(n_chunks):
            r0 = rows_cb * tau + batch * c
            for j in range(n_tiles):
                slab_out[j, pl.ds(SSM_CHUNK * c + tau, batch, stride=pitch), :] = (
                    s[r0:r0 + batch, LANE_TILE * j:LANE_TILE * (j + 1)])
    for b in range(batch):
        s_ref[b] = jnp.concatenate(
            [slab_out[j, pitch * b:pitch * b + steps, :] for j in range(n_tiles)], axis=1)


def _ssm(x, w_all, consts, steps):
    batch, l, d = x.shape
    n_tiles = SSM_WIDTH // LANE_TILE
    rows_cb = steps // SSM_CHUNK * batch
    kern = functools.partial(_ssm_kernel, steps=steps, batch=batch)
    slab = pltpu.VMEM((n_tiles, (steps + 8) * batch, LANE_TILE), F32)
    state_cols = SSM_GROUPS * STATE_LANES
    return pl.pallas_call(
        kern,
        grid=(l // steps,),
        in_specs=[pl.BlockSpec((batch, steps, d), lambda i: (0, i, 0)), _const_spec(consts[0].shape),
                  _col_block_spec(d, SSM_WIDTH, U_COL_BLOCK)]
        + [_const_spec(c.shape) for c in consts[1:]],
        out_specs=[pl.BlockSpec((batch, steps, SSM_WIDTH), lambda i: (0, i, 0)),
                   pl.BlockSpec((batch, steps, d), lambda i: (0, i, 0))],
        out_shape=[jax.ShapeDtypeStruct((batch, l, SSM_WIDTH), F32),
                   jax.ShapeDtypeStruct((batch, l, d), BF16)],
        scratch_shapes=[
            slab, slab,
            pltpu.VMEM((steps * batch, SSM_WIDTH), F32),
            pltpu.VMEM((rows_cb, state_cols), F32), pltpu.VMEM((rows_cb, state_cols), F32),
            pltpu.VMEM((rows_cb, state_cols), F32),
            pltpu.VMEM((batch, state_cols), F32), pltpu.VMEM((batch, state_cols), F32),
        ],
        compiler_params=pltpu.CompilerParams(
            dimension_semantics=("arbitrary",), vmem_limit_bytes=VMEM_LIMIT_BYTES),
        name="ssm",
    )(x, consts[0], w_all, *consts[1:])


PAIR = 2 * CHUNK
WINDOW = BAND + CHUNK
N_PAIRS = N_HEADS // 2
U_COL_BLOCK = 4
MAIN_W_IN_BLOCKS = ((ATTN_WIDTH, 0), (ATTN_WIDTH, 1), (ATTN_WIDTH, 2), (ATTN_WIDTH, 3),
                    (SSM_WIDTH, 5), (2 * ATTN_WIDTH, 3), (2 * ATTN_WIDTH, 4))


class _AttnRefs:
    def __init__(self, bias_ref, qm_s, k_s, vt_s, yat_s, s_bufs, p_bufs, r_bufs):
        self.bias_ref, self.qm_s, self.k_s, self.vt_s, self.yat_s = bias_ref, qm_s, k_s, vt_s, yat_s
        self.s_bufs, self.p_bufs, self.r_bufs = s_bufs, p_bufs, r_bufs


def _unit_index(k):
    if isinstance(k, int):
        return k // N_PAIRS, k % N_PAIRS
    return lax.shift_right_logical(k, 2), lax.bitwise_and(k, N_PAIRS - 1)


def _row_start(idx, size):
    return idx * size if isinstance(idx, int) else pl.multiple_of(idx * size, size)


def _nt_dot(a, b):
    return lax.dot_general(a, b, (((1,), (1,)), ((), ())), preferred_element_type=F32)


def _scores_stage(r, k, slot, masked_start):
    m, pr = _unit_index(k)
    kwin = r.k_s[pr, pl.ds(_row_start(m, PAIR), WINDOW), :]
    s = _nt_dot(kwin, r.qm_s[pr, m]) + r.bias_ref[pr]
    row = lax.broadcasted_iota(jnp.int32, s.shape, 0)
    s = jnp.where(row >= masked_start * (HIST - PAIR * m), s, NEG_INF)
    r.s_bufs[slot][...] = s


def _softmax_stage(r, slot):
    s = r.s_bufs[slot][...]
    p = jnp.exp2(s - jnp.max(s, axis=0, keepdims=True))
    r.r_bufs[slot][...] = 1.0 / jnp.sum(p, axis=0, keepdims=True)
    r.p_bufs[slot][...] = p.astype(BF16)


def _values_stage(r, k, slot):
    m, pr = _unit_index(k)
    vwin = r.vt_s[pl.ds(_row_start(pr, PAIR), PAIR), pl.ds(_row_start(m, PAIR), WINDOW)]
    o = jnp.dot(vwin, r.p_bufs[slot][...], preferred_element_type=F32) * r.r_bufs[slot][...]
    r.yat_s[m, pl.ds(_row_start(pr, PAIR), PAIR), :] = jnp.concatenate(
        [o[:HEAD_DIM, :PAIR], o[HEAD_DIM:, PAIR:]], axis=0)


def _attention(r, n_units, masked_start):
    _scores_stage(r, 0, 0, masked_start)
    _scores_stage(r, 1, 1, masked_start)
    _softmax_stage(r, 0)

    def body(kk, carry):
        k = 2 * kk + 2
        _scores_stage(r, k, 0, masked_start)
        _softmax_stage(r, 1)
        _values_stage(r, k - 2, 0)
        _scores_stage(r, k + 1, 1, masked_start)
        _softmax_stage(r, 0)
        _values_stage(r, k - 1, 1)
        return carry

    lax.fori_loop(0, (n_units - 2) // 2, body, 0, unroll=True)
    _softmax_stage(r, 1)
    _values_stage(r, n_units - 2, 0)
    _values_stage(r, n_units - 1, 1)


def _main_kernel(x_ref, h_ref, s_ref, wq_ref, wk_ref, wv_ref, wza_ref, wzs_ref, wga_ref, wgs_ref,
                 gb_ref, band_bias_ref, wao_ref, wso_ref, wo_ref, fgain_ref, o_ref,
                 k_s, vt_s, qm_s, yat_s, s_buf0, s_buf1, p_buf0, p_buf1, r_buf0, r_buf1,
                 bias_s, *, tm):
    i = pl.program_id(1)
    n_m = tm // PAIR

    @pl.when((pl.program_id(0) == 0) & (i == 0))
    def _():
        masked = jnp.full((CHUNK, CHUNK), NEG_INF, F32)
        for pr in range(N_PAIRS):
            blocks = []
            for head in (2 * pr, 2 * pr + 1):
                band = band_bias_ref[head]
                blocks.append(jnp.concatenate([band, masked], axis=1))
                blocks.append(jnp.concatenate([masked, band], axis=1))
            bias_s[pr] = jnp.concatenate(blocks, axis=0).T
        k_s[:, 0:HIST, :] = jnp.zeros((N_PAIRS, HIST, PAIR), BF16)
        vt_s[:, 0:HIST] = jnp.zeros((ATTN_WIDTH, HIST), BF16)

    h = h_ref[...]

    q = jnp.dot(h, wq_ref[...], preferred_element_type=F32)
    lane = lax.broadcasted_iota(jnp.int32, (tm, PAIR), 1)
    for pr in range(N_PAIRS):
        qp = q[:, PAIR * pr:PAIR * (pr + 1)]
        lo = jnp.where(lane < HEAD_DIM, qp, 0.0).astype(BF16)
        hi = jnp.where(lane >= HEAD_DIM, qp, 0.0).astype(BF16)
        for m in range(n_m):
            qm_s[pr, m, 0:PAIR, :] = lo[PAIR * m:PAIR * (m + 1)]
            qm_s[pr, m, PAIR:2 * PAIR, :] = hi[PAIR * m:PAIR * (m + 1)]

    k_new = jnp.dot(h, wk_ref[...], preferred_element_type=F32)
    vt_new = jnp.dot(h, wv_ref[...], preferred_element_type=F32).T

    vt_s[:, HIST:HIST + tm] = vt_new.astype(BF16)
    for pr in range(N_PAIRS):
        k_s[pr, HIST:HIST + tm, :] = k_new[:, PAIR * pr:PAIR * (pr + 1)].astype(BF16)

    refs = _AttnRefs(bias_s, qm_s, k_s, vt_s, yat_s,
                     (s_buf0, s_buf1), (p_buf0, p_buf1), (r_buf0, r_buf1))

    _attention(refs, n_m * N_PAIRS, (i == 0).astype(jnp.int32))

    za = jnp.dot(h, wza_ref[...], preferred_element_type=F32)
    yat = jnp.concatenate([yat_s[m] for m in range(n_m)], axis=1)
    ya = yat.T * jax.nn.silu(za)
    yao = jnp.dot(ya.astype(BF16), wao_ref[...], preferred_element_type=F32)
    zs = jnp.dot(h, wzs_ref[...], preferred_element_type=F32)
    ys = s_ref[...] * jax.nn.silu(zs)
    yso = jnp.dot(ys.astype(BF16), wso_ref[...], preferred_element_type=F32)
    d = x_ref.shape[-1]
    ga = jnp.dot(h, wga_ref[...], preferred_element_type=F32) + gb_ref[:, :d]
    gs = jnp.dot(h, wgs_ref[...], preferred_element_type=F32) + gb_ref[:, d:]
    merged = jax.nn.sigmoid(ga) * yao + jax.nn.sigmoid(gs) * yso
    xn = x_ref[...] + jnp.dot(merged.astype(BF16), wo_ref[...], preferred_element_type=F32)
    o_ref[...] = _rms_norm(xn, fgain_ref[...])
    k_s[:, 0:HIST, :] = k_s[:, tm:tm + HIST, :]
    vt_s[:, 0:HIST] = vt_s[:, tm:tm + HIST]


def _main(x, h, s, w_all, consts):
    b, l, d = x.shape
    tm = MAIN_TILE
    n_m = tm // PAIR
    kern = functools.partial(_main_kernel, tm=tm)
    unit_rows = 2 * PAIR
    return pl.pallas_call(
        kern,
        grid=(b, l // tm),
        in_specs=[
            pl.BlockSpec((None, tm, d), lambda bi, i: (bi, i, 0)),
            pl.BlockSpec((None, tm, d), lambda bi, i: (bi, i, 0)),
            pl.BlockSpec((None, tm, SSM_WIDTH), lambda bi, i: (bi, i, 0)),
        ] + [_col_block_spec(d, width, index) for width, index in MAIN_W_IN_BLOCKS]
        + [_const_spec(c.shape) for c in consts],
        out_specs=pl.BlockSpec((None, tm, d), lambda bi, i: (bi, i, 0)),
        out_shape=jax.ShapeDtypeStruct((b, l, d), F32),
        scratch_shapes=[
            pltpu.VMEM((N_PAIRS, HIST + tm, PAIR), BF16),
            pltpu.VMEM((ATTN_WIDTH, HIST + tm), BF16),
            pltpu.VMEM((N_PAIRS, n_m, unit_rows, PAIR), BF16),
            pltpu.VMEM((n_m, ATTN_WIDTH, PAIR), F32),
            pltpu.VMEM((WINDOW, unit_rows), F32), pltpu.VMEM((WINDOW, unit_rows), F32),
            pltpu.VMEM((WINDOW, unit_rows), BF16), pltpu.VMEM((WINDOW, unit_rows), BF16),
            pltpu.VMEM((1, unit_rows), F32), pltpu.VMEM((1, unit_rows), F32),
            pltpu.VMEM((N_PAIRS, WINDOW, unit_rows), F32),
        ],
        compiler_params=pltpu.CompilerParams(
            dimension_semantics=("arbitrary", "arbitrary"), vmem_limit_bytes=VMEM_LIMIT_BYTES),
        name="main_block",
    )(x, h, s, *([w_all] * len(MAIN_W_IN_BLOCKS)), *consts)


def _ssm_chunk_params(a_re, a_im, log_dt, b_re, b_im, c_re, c_im):
    hi = lax.Precision.HIGHEST
    n = SSM_CHUNK
    dt = jnp.exp(log_dt)[:, None]
    k = jnp.arange(n + 1, dtype=F32)[None, :, None]
    mag = jnp.exp((a_re * dt)[:, None, :] * k)
    pw_re = mag * jnp.cos((a_im * dt)[:, None, :] * k)
    pw_im = mag * jnp.sin((a_im * dt)[:, None, :] * k)
    n_re = pw_re[:, 1] - 1.0
    n_im = pw_im[:, 1]
    den = a_re * a_re + a_im * a_im
    f_re = ((n_re * a_re + n_im * a_im) / den)[:, None, :]
    f_im = ((n_im * a_re - n_re * a_im) / den)[:, None, :]
    bt_re, bt_im = jnp.swapaxes(b_re, 1, 2), jnp.swapaxes(b_im, 1, 2)
    bb_re = f_re * bt_re - f_im * bt_im
    bb_im = f_re * bt_im + f_im * bt_re
    w_re = pw_re[:, :n, None, :] * bb_re[:, None] - pw_im[:, :n, None, :] * bb_im[:, None]
    w_im = pw_re[:, :n, None, :] * bb_im[:, None] + pw_im[:, :n, None, :] * bb_re[:, None]
    kern = (jnp.einsum("gop,gkcp->gcko", c_re, w_re, precision=hi)
            - jnp.einsum("gop,gkcp->gcko", c_im, w_im, precision=hi))
    kern = kern.reshape(SSM_GROUPS, SSM_GROUP, LANE_TILE)
    zeros = jnp.zeros_like(kern)
    m = jnp.stack([jnp.concatenate([zeros[:, :, :SSM_GROUP * sig], kern[:, :, :SSM_GROUP * (n - sig)]],
                                   axis=2) for sig in range(n)], axis=1)
    m = m.reshape(SSM_GROUPS, LANE_TILE, LANE_TILE).astype(BF16)
    ws_re = w_re[:, ::-1].reshape(SSM_GROUPS, LANE_TILE, SSM_STATE)
    ws_im = w_im[:, ::-1].reshape(SSM_GROUPS, LANE_TILE, SSM_STATE)
    ws = jnp.concatenate([ws_re, ws_im, ws_im, ws_re], axis=2).astype(BF16)
    ct_re, ct_im = jnp.swapaxes(c_re, 1, 2)[:, :, None, :], jnp.swapaxes(c_im, 1, 2)[:, :, None, :]
    pt_re = jnp.swapaxes(pw_re[:, 1:], 1, 2)[..., None]
    pt_im = jnp.swapaxes(pw_im[:, 1:], 1, 2)[..., None]
    p_re = (ct_re * pt_re - ct_im * pt_im).reshape(SSM_GROUPS, SSM_STATE, LANE_TILE)
    p_im = (ct_re * pt_im + ct_im * pt_re).reshape(SSM_GROUPS, SSM_STATE, LANE_TILE)
    p = jnp.concatenate([p_re, -p_im], axis=1).astype(BF16)
    aa = jnp.concatenate([pw_re[:, n], pw_re[:, n]], axis=1).reshape(1, -1)
    ab = jnp.concatenate([-pw_im[:, n], pw_im[:, n]], axis=1).reshape(1, -1)
    return m, ws, p, aa, ab


def _band_bias(rel_bias):
    n_rel = rel_bias.shape[1]
    ext = jnp.concatenate(
        [rel_bias, jnp.broadcast_to(rel_bias[:, -1:], (N_HEADS, CHUNK + BAND - 1 - n_rel))], axis=1)
    by_rev_k = jnp.stack([ext[:, q:q + BAND] for q in range(CHUNK)], axis=1)
    return by_rev_k[:, :, ::-1] * LOG2E


def kernel(x, norm_gain, w_in, rel_bias, ssm_a_re, ssm_a_im, ssm_log_dt, ssm_b_re, ssm_b_im,
           ssm_c_re, ssm_c_im, ssm_d, w_glu, b_glu, w_attn_out, w_ssm_out, gate_bias, w_out,
           final_gain):
    assert norm_gain.shape[0] == 1, "single-layer block"
    b, l, d = x.shape
    aw, sw = ATTN_WIDTH, SSM_WIDTH
    col_scale = jnp.where(jnp.arange(w_in.shape[-1]) < aw, LOG2E / math.sqrt(HEAD_DIM), 1.0)
    w_all = (w_in[0] * col_scale).astype(BF16)
    gain = norm_gain[0].reshape(1, d)

    m, ws, p, aa, ab = _ssm_chunk_params(
        ssm_a_re[0], ssm_a_im[0], ssm_log_dt[0], ssm_b_re[0], ssm_b_im[0], ssm_c_re[0], ssm_c_im[0])
    s, h = _ssm(x, w_all, [gain, m, ws, p, aa, ab, ssm_d[0].reshape(1, sw),
                        w_glu[0].astype(BF16), b_glu[0].reshape(1, 2 * sw)], steps=SSM_STEPS)

    consts = [gate_bias[0].reshape(1, 2 * d), _band_bias(rel_bias[0]),
              w_attn_out[0].astype(BF16), w_ssm_out[0].astype(BF16), w_out[0].astype(BF16),
              final_gain.reshape(1, d)]
    return _main(x, h, s, w_all, consts)
```

```python
import functools
import math

import jax
import jax.numpy as jnp
from jax import lax
from jax.experimental import pallas as pl
from jax.experimental.pallas import tpu as pltpu

F32 = jnp.float32
BF16 = jnp.bfloat16

CHUNK = 64
LEFT_CHUNKS = 8
BAND = (LEFT_CHUNKS + 1) * CHUNK
HIST = LEFT_CHUNKS * CHUNK
N_HEADS = 8
HEAD_DIM = 64
ATTN_WIDTH = N_HEADS * HEAD_DIM
SSM_WIDTH = 512
SSM_GROUP = 16
SSM_GROUPS = SSM_WIDTH // SSM_GROUP
SSM_STATE = 64
NORM_EPS = 1e-6
NEG_INF = -1e30
LOG2E = math.log2(math.e)

V7X_VMEM_BYTES = 64 * 1024 * 1024
VMEM_LIMIT_BYTES = V7X_VMEM_BYTES - 8 * 1024 * 1024
SSM_STEPS = 128
MAIN_TILE = HIST
SUB_TILES = 2


def _rms_norm(x, gain):
    inv = lax.rsqrt(jnp.mean(x * x, axis=-1, keepdims=True) + NORM_EPS)
    return x * inv * gain


def _const_spec(shape):
    zeros = (0,) * len(shape)
    return pl.BlockSpec(shape, lambda *_: zeros, pipeline_mode=pl.Buffered(1))


def _col_block_spec(rows, width, index):
    return pl.BlockSpec((rows, width), lambda *_: (0, index), pipeline_mode=pl.Buffered(1))


LANE_TILE = 128
SSM_CHUNK = LANE_TILE // SSM_GROUP
GROUPS_PER_TILE = LANE_TILE // SSM_GROUP
STATE_LANES = 2 * SSM_STATE


def _block_transpose(arrs):
    n = len(arrs)
    lane_block = lax.broadcasted_iota(jnp.int32, arrs[0].shape, 1) // SSM_GROUP
    rot = [a if i == 0 else pltpu.roll(a, SSM_GROUP * i, axis=1) for i, a in enumerate(arrs)]
    x = [rot[-k % n] for k in range(n)]
    for d in (1, 2, 4):
        has_bit = (lane_block & d) != 0
        x = [jnp.where(has_bit, x[(k - d) % n], x[k]) for k in range(n)]
    return [a if k == 0 else pltpu.roll(a, LANE_TILE - SSM_GROUP * k, axis=1) for k, a in enumerate(x)]


def _ssm_kernel(x_ref, gain_ref, wu_ref, m_ref, ws_ref, p_ref, aa_ref, ab_ref, d_ref, wglu_ref, bglu_ref,
                s_ref, h_ref, slab_in, slab_out, ur_s, sc_s, scw_s, hp_s, st, st_sw,
                *, steps, batch):
    i = pl.program_id(0)
    n_chunks = steps // SSM_CHUNK
    rows_cb = n_chunks * batch
    pitch = steps + 8
    n_tiles = SSM_WIDTH // LANE_TILE

    @pl.when(i == 0)
    def _():
        st[...] = jnp.zeros_like(st)
        st_sw[...] = jnp.zeros_like(st_sw)

    x = x_ref[...].reshape(batch * steps, x_ref.shape[-1])
    h = _rms_norm(x, gain_ref[...]).astype(BF16)
    h_ref[...] = h.reshape(h_ref.shape)
    u_bt = jnp.dot(h, wu_ref[...], preferred_element_type=F32)

    for j in range(n_tiles):
        for b in range(batch):
            slab_in[j, pitch * b:pitch * b + steps, :] = (
                u_bt[steps * b:steps * (b + 1), LANE_TILE * j:LANE_TILE * (j + 1)])
    group_lanes = GROUPS_PER_TILE * STATE_LANES
    for j in range(n_tiles):
        lanes_j = slice(LANE_TILE * j, LANE_TILE * (j + 1))
        by_step = [
            jnp.concatenate([slab_in[j, pl.ds(SSM_CHUNK * c + tau, batch, stride=pitch), :]
                             for c in range(n_chunks)], axis=0)
            for tau in range(SSM_CHUNK)]
        for tau in range(SSM_CHUNK):
            ur_s[rows_cb * tau:rows_cb * (tau + 1), lanes_j] = by_step[tau]
        by_group = _block_transpose(by_step)

        intra = []
        for g8 in range(GROUPS_PER_TILE):
            g = GROUPS_PER_TILE * j + g8
            ug = by_group[g8].astype(BF16)
            intra.append(jnp.dot(ug, m_ref[g], preferred_element_type=F32))
            state = jnp.dot(ug, ws_ref[g], preferred_element_type=F32)
            sc_s[:, STATE_LANES * g:STATE_LANES * (g + 1)] = state[:, :STATE_LANES]
            scw_s[:, STATE_LANES * g:STATE_LANES * (g + 1)] = state[:, STATE_LANES:]

        sl = slice(group_lanes * j, group_lanes * (j + 1))
        aa = jnp.broadcast_to(aa_ref[:, sl], (batch, group_lanes))
        ab = jnp.broadcast_to(ab_ref[:, sl], (batch, group_lanes))
        hh, hw = st[:, sl], st_sw[:, sl]
        for c in range(n_chunks):
            rows = slice(batch * c, batch * (c + 1))
            hp_s[rows, sl] = hh
            hh, hw = (aa * hh + ab * hw + sc_s[rows, sl], aa * hw - ab * hh + scw_s[rows, sl])
        st[:, sl] = hh
        st_sw[:, sl] = hw

        by_group = []
        for g8 in range(GROUPS_PER_TILE):
            g = GROUPS_PER_TILE * j + g8
            hp = hp_s[:, STATE_LANES * g:STATE_LANES * (g + 1)].astype(BF16)
            by_group.append(intra[g8] + jnp.dot(hp, p_ref[g], preferred_element_type=F32))
        by_step = _block_transpose(by_group)
        for tau in range(SSM_CHUNK):
            rows = slice(rows_cb * tau, rows_cb * (tau + 1))
            ur_s[rows, lanes_j] = by_step[tau] + d_ref[:, lanes_j] * ur_s[rows, lanes_j]

    g_act = jax.nn.gelu(ur_s[...])
    z = jnp.dot(g_act.astype(BF16), wglu_ref[...], preferred_element_type=F32) + bglu_ref[...]
    s = z[:, :SSM_WIDTH] * jax.nn.sigmoid(z[:, SSM_WIDTH:])

    for tau in range(SSM_CHUNK):
        for c in range(n_chunks):
            r0 = rows_cb * tau + batch * c
            for j in range(n_tiles):
                slab_out[j, pl.ds(SSM_CHUNK * c + tau, batch, stride=pitch), :] = (
                    s[r0:r0 + batch, LANE_TILE * j:LANE_TILE * (j + 1)])
    for b in range(batch):
        s_ref[b] = jnp.concatenate(
            [slab_out[j, pitch * b:pitch * b + steps, :] for j in range(n_tiles)], axis=1)


def _ssm(x, w_all, consts, steps):
    batch, l, d = x.shape
    n_tiles = SSM_WIDTH // LANE_TILE
    rows_cb = steps // SSM_CHUNK * batch
    kern = functools.partial(_ssm_kernel, steps=steps, batch=batch)
    slab = pltpu.VMEM((n_tiles, (steps + 8) * batch, LANE_TILE), F32)
    state_cols = SSM_GROUPS * STATE_LANES
    return pl.pallas_call(
        kern,
        grid=(l // steps,),
        in_specs=[pl.BlockSpec((batch, steps, d), lambda i: (0, i, 0)), _const_spec(consts[0].shape),
                  _col_block_spec(d, SSM_WIDTH, U_COL_BLOCK)]
        + [_const_spec(c.shape) for c in consts[1:]],
        out_specs=[pl.BlockSpec((batch, steps, SSM_WIDTH), lambda i: (0, i, 0)),
                   pl.BlockSpec((batch, steps, d), lambda i: (0, i, 0))],
        out_shape=[jax.ShapeDtypeStruct((batch, l, SSM_WIDTH), F32),
                   jax.ShapeDtypeStruct((batch, l, d), BF16)],
        scratch_shapes=[
            slab, slab,
            pltpu.VMEM((steps * batch, SSM_WIDTH), F32),
            pltpu.VMEM((rows_cb, state_cols), F32), pltpu.VMEM((rows_cb, state_cols), F32),
            pltpu.VMEM((rows_cb, state_cols), F32),
            pltpu.VMEM((batch, state_cols), F32), pltpu.VMEM((batch, state_cols), F32),
        ],
        compiler_params=pltpu.CompilerParams(
            dimension_semantics=("arbitrary",), vmem_limit_bytes=VMEM_LIMIT_BYTES),
        name="ssm",
    )(x, consts[0], w_all, *consts[1:])


PAIR = 2 * CHUNK
WINDOW = BAND + CHUNK
N_PAIRS = N_HEADS // 2
U_COL_BLOCK = 4
MAIN_W_IN_BLOCKS = ((ATTN_WIDTH, 0), (ATTN_WIDTH, 1), (ATTN_WIDTH, 2), (ATTN_WIDTH, 3),
                    (SSM_WIDTH, 5), (2 * ATTN_WIDTH, 3), (2 * ATTN_WIDTH, 4))


class _AttnRefs:
    def __init__(self, bias_ref, qm_s, k_s, vt_s, yat_s, s_bufs, p_bufs, r_bufs):
        self.bias_ref, self.qm_s, self.k_s, self.vt_s, self.yat_s = bias_ref, qm_s, k_s, vt_s, yat_s
        self.s_bufs, self.p_bufs, self.r_bufs = s_bufs, p_bufs, r_bufs
        self.tm = MAIN_TILE


def _unit_index(k):
    if isinstance(k, int):
        return k // N_PAIRS, k % N_PAIRS
    return lax.shift_right_logical(k, 2), lax.bitwise_and(k, N_PAIRS - 1)


def _row_start(idx, size):
    return idx * size if isinstance(idx, int) else pl.multiple_of(idx * size, size)


def _nt_dot(a, b):
    return lax.dot_general(a, b, (((1,), (1,)), ((), ())), preferred_element_type=F32)


def _scores_stage(r, k, slot, masked_start, sub):
    m, pr = _unit_index(k)
    kwin = r.k_s[pr, pl.ds(r.tm * sub + _row_start(m, PAIR), WINDOW), :]
    s = _nt_dot(kwin, r.qm_s[sub, pr, m]) + r.bias_ref[pr]
    if not (isinstance(masked_start, int) and masked_start == 0):
        row = lax.broadcasted_iota(jnp.int32, s.shape, 0)
        s = jnp.where(row >= masked_start * (HIST - PAIR * m), s, NEG_INF)
    r.s_bufs[slot][...] = s


def _softmax_stage(r, slot):
    s = r.s_bufs[slot][...]
    p = jnp.exp2(s - jnp.max(s, axis=0, keepdims=True))
    r.r_bufs[slot][...] = 1.0 / jnp.sum(p, axis=0, keepdims=True)
    r.p_bufs[slot][...] = p.astype(BF16)


def _values_stage(r, k, slot, sub):
    m, pr = _unit_index(k)
    vwin = r.vt_s[pl.ds(_row_start(pr, PAIR), PAIR), pl.ds(r.tm * sub + _row_start(m, PAIR), WINDOW)]
    o = jnp.dot(vwin, r.p_bufs[slot][...], preferred_element_type=F32) * r.r_bufs[slot][...]
    r.yat_s[sub, m, pl.ds(_row_start(pr, PAIR), PAIR), :] = jnp.concatenate(
        [o[:HEAD_DIM, :PAIR], o[HEAD_DIM:, PAIR:]], axis=0)


def _attention(r, n_units, masked_start, sub):
    for k in range(n_units + 2):
        if k < n_units:
            _scores_stage(r, k, k % 2, masked_start, sub)
        if 1 <= k <= n_units:
            _softmax_stage(r, (k - 1) % 2)
        if k >= 2:
            _values_stage(r, k - 2, (k - 2) % 2, sub)


def _main_kernel(x_ref, h_ref, s_ref, wq_ref, wk_ref, wv_ref, wza_ref, wzs_ref, wga_ref, wgs_ref,
                 gb_ref, band_bias_ref, wao_ref, wso_ref, wo_ref, fgain_ref, o_ref,
                 k_s, vt_s, qm_s, yat_s, s_buf0, s_buf1, p_buf0, p_buf1, r_buf0, r_buf1,
                 bias_s, *, tm):
    i = pl.program_id(1)
    n_m = tm // PAIR

    @pl.when((pl.program_id(0) == 0) & (i == 0))
    def _():
        masked = jnp.full((CHUNK, CHUNK), NEG_INF, F32)
        for pr in range(N_PAIRS):
            blocks = []
            for head in (2 * pr, 2 * pr + 1):
                band = band_bias_ref[head]
                blocks.append(jnp.concatenate([band, masked], axis=1))
                blocks.append(jnp.concatenate([masked, band], axis=1))
            bias_s[pr] = jnp.concatenate(blocks, axis=0).T
        k_s[:, 0:HIST, :] = jnp.zeros((N_PAIRS, HIST, PAIR), BF16)
        vt_s[:, 0:HIST] = jnp.zeros((ATTN_WIDTH, HIST), BF16)

    refs = _AttnRefs(bias_s, qm_s, k_s, vt_s, yat_s,
                     (s_buf0, s_buf1), (p_buf0, p_buf1), (r_buf0, r_buf1))
    d = x_ref.shape[-1]
    lane = lax.broadcasted_iota(jnp.int32, (tm, PAIR), 1)
    for sub in range(SUB_TILES):
        rows = slice(tm * sub, tm * (sub + 1))
        pos = slice(HIST + tm * sub, HIST + tm * (sub + 1))
        h = h_ref[rows, :]

        q = jnp.dot(h, wq_ref[...], preferred_element_type=F32)
        for pr in range(N_PAIRS):
            qp = q[:, PAIR * pr:PAIR * (pr + 1)]
            lo = jnp.where(lane < HEAD_DIM, qp, 0.0).astype(BF16)
            hi = jnp.where(lane >= HEAD_DIM, qp, 0.0).astype(BF16)
            for m in range(n_m):
                qm_s[sub, pr, m, 0:PAIR, :] = lo[PAIR * m:PAIR * (m + 1)]
                qm_s[sub, pr, m, PAIR:2 * PAIR, :] = hi[PAIR * m:PAIR * (m + 1)]

        k_new = jnp.dot(h, wk_ref[...], preferred_element_type=F32)
        vt_new = jnp.dot(h, wv_ref[...], preferred_element_type=F32).T
        vt_s[:, pos] = vt_new.astype(BF16)
        for pr in range(N_PAIRS):
            k_s[pr, pos, :] = k_new[:, PAIR * pr:PAIR * (pr + 1)].astype(BF16)

        masked_start = (i == 0).astype(jnp.int32) if sub == 0 else 0
        _attention(refs, n_m * N_PAIRS, masked_start, sub)

        za = jnp.dot(h, wza_ref[...], preferred_element_type=F32)
        yat = jnp.concatenate([yat_s[sub, m] for m in range(n_m)], axis=1)
        ya = yat.T * jax.nn.silu(za)
        yao = jnp.dot(ya.astype(BF16), wao_ref[...], preferred_element_type=F32)
        zs = jnp.dot(h, wzs_ref[...], preferred_element_type=F32)
        ys = s_ref[rows, :] * jax.nn.silu(zs)
        yso = jnp.dot(ys.astype(BF16), wso_ref[...], preferred_element_type=F32)
        ga = jnp.dot(h, wga_ref[...], preferred_element_type=F32) + gb_ref[:, :d]
        gs = jnp.dot(h, wgs_ref[...], preferred_element_type=F32) + gb_ref[:, d:]
        merged = jax.nn.sigmoid(ga) * yao + jax.nn.sigmoid(gs) * yso
        xn = x_ref[rows, :] + jnp.dot(merged.astype(BF16), wo_ref[...], preferred_element_type=F32)
        o_ref[rows, :] = _rms_norm(xn, fgain_ref[...])
    span = SUB_TILES * tm
    k_s[:, 0:HIST, :] = k_s[:, span:span + HIST, :]
    vt_s[:, 0:HIST] = vt_s[:, span:span + HIST]


def _main(x, h, s, w_all, consts):
    b, l, d = x.shape
    tm = MAIN_TILE
    n_m = tm // PAIR
    kern = functools.partial(_main_kernel, tm=tm)
    unit_rows = 2 * PAIR
    span = SUB_TILES * tm
    return pl.pallas_call(
        kern,
        grid=(b, l // span),
        in_specs=[
            pl.BlockSpec((None, span, d), lambda bi, i: (bi, i, 0)),
            pl.BlockSpec((None, span, d), lambda bi, i: (bi, i, 0)),
            pl.BlockSpec((None, span, SSM_WIDTH), lambda bi, i: (bi, i, 0)),
        ] + [_col_block_spec(d, width, index) for width, index in MAIN_W_IN_BLOCKS]
        + [_const_spec(c.shape) for c in consts],
        out_specs=pl.BlockSpec((None, span, d), lambda bi, i: (bi, i, 0)),
        out_shape=jax.ShapeDtypeStruct((b, l, d), F32),
        scratch_shapes=[
            pltpu.VMEM((N_PAIRS, HIST + span, PAIR), BF16),
            pltpu.VMEM((ATTN_WIDTH, HIST + span), BF16),
            pltpu.VMEM((SUB_TILES, N_PAIRS, n_m, unit_rows, PAIR), BF16),
            pltpu.VMEM((SUB_TILES, n_m, ATTN_WIDTH, PAIR), F32),
            pltpu.VMEM((WINDOW, unit_rows), F32), pltpu.VMEM((WINDOW, unit_rows), F32),
            pltpu.VMEM((WINDOW, unit_rows), BF16), pltpu.VMEM((WINDOW, unit_rows), BF16),
            pltpu.VMEM((1, unit_rows), F32), pltpu.VMEM((1, unit_rows), F32),
            pltpu.VMEM((N_PAIRS, WINDOW, unit_rows), F32),
        ],
        compiler_params=pltpu.CompilerParams(
            dimension_semantics=("arbitrary", "arbitrary"), vmem_limit_bytes=VMEM_LIMIT_BYTES),
        name="main_block",
    )(x, h, s, *([w_all] * len(MAIN_W_IN_BLOCKS)), *consts)


def _ssm_chunk_params(a_re, a_im, log_dt, b_re, b_im, c_re, c_im):
    hi = lax.Precision.HIGHEST
    n = SSM_CHUNK
    dt = jnp.exp(log_dt)[:, None]
    k = jnp.arange(n + 1, dtype=F32)[None, :, None]
    mag = jnp.exp((a_re * dt)[:, None, :] * k)
    pw_re = mag * jnp.cos((a_im * dt)[:, None, :] * k)
    pw_im = mag * jnp.sin((a_im * dt)[:, None, :] * k)
    n_re = pw_re[:, 1] - 1.0
    n_im = pw_im[:, 1]
    den = a_re * a_re + a_im * a_im
    f_re = ((n_re * a_re + n_im * a_im) / den)[:, None, :]
    f_im = ((n_im * a_re - n_re * a_im) / den)[:, None, :]
    bt_re, bt_im = jnp.swapaxes(b_re, 1, 2), jnp.swapaxes(b_im, 1, 2)
    bb_re = f_re * bt_re - f_im * bt_im
    bb_im = f_re * bt_im + f_im * bt_re
    w_re = pw_re[:, :n, None, :] * bb_re[:, None] - pw_im[:, :n, None, :] * bb_im[:, None]
    w_im = pw_re[:, :n, None, :] * bb_im[:, None] + pw_im[:, :n, None, :] * bb_re[:, None]
    kern = (jnp.einsum("gop,gkcp->gcko", c_re, w_re, precision=hi)
            - jnp.einsum("gop,gkcp->gcko", c_im, w_im, precision=hi))
    kern = kern.reshape(SSM_GROUPS, SSM_GROUP, LANE_TILE)
    zeros = jnp.zeros_like(kern)
    m = jnp.stack([jnp.concatenate([zeros[:, :, :SSM_GROUP * sig], kern[:, :, :SSM_GROUP * (n - sig)]],
                                   axis=2) for sig in range(n)], axis=1)
    m = m.reshape(SSM_GROUPS, LANE_TILE, LANE_TILE).astype(BF16)
    ws_re = w_re[:, ::-1].reshape(SSM_GROUPS, LANE_TILE, SSM_STATE)
    ws_im = w_im[:, ::-1].reshape(SSM_GROUPS, LANE_TILE, SSM_STATE)
    ws = jnp.concatenate([ws_re, ws_im, ws_im, ws_re], axis=2).astype(BF16)
    ct_re, ct_im = jnp.swapaxes(c_re, 1, 2)[:, :, None, :], jnp.swapaxes(c_im, 1, 2)[:, :, None, :]
    pt_re = jnp.swapaxes(pw_re[:, 1:], 1, 2)[..., None]
    pt_im = jnp.swapaxes(pw_im[:, 1:], 1, 2)[..., None]
    p_re = (ct_re * pt_re - ct_im * pt_im).reshape(SSM_GROUPS, SSM_STATE, LANE_TILE)
    p_im = (ct_re * pt_im + ct_im * pt_re).reshape(SSM_GROUPS, SSM_STATE, LANE_TILE)
    p = jnp.concatenate([p_re, -p_im], axis=1).astype(BF16)
    aa = jnp.concatenate([pw_re[:, n], pw_re[:, n]], axis=1).reshape(1, -1)
    ab = jnp.concatenate([-pw_im[:, n], pw_im[:, n]], axis=1).reshape(1, -1)
    return m, ws, p, aa, ab


def _band_bias(rel_bias):
    n_rel = rel_bias.shape[1]
    ext = jnp.concatenate(
        [rel_bias, jnp.broadcast_to(rel_bias[:, -1:], (N_HEADS, CHUNK + BAND - 1 - n_rel))], axis=1)
    by_rev_k = jnp.stack([ext[:, q:q + BAND] for q in range(CHUNK)], axis=1)
    return by_rev_k[:, :, ::-1] * LOG2E


def kernel(x, norm_gain, w_in, rel_bias, ssm_a_re, ssm_a_im, ssm_log_dt, ssm_b_re, ssm_b_im,
           ssm_c_re, ssm_c_im, ssm_d, w_glu, b_glu, w_attn_out, w_ssm_out, gate_bias, w_out,
           final_gain):
    assert norm_gain.shape[0] == 1, "single-layer block"
    b, l, d = x.shape
    aw, sw = ATTN_WIDTH, SSM_WIDTH
    col_scale = jnp.where(jnp.arange(w_in.shape[-1]) < aw, LOG2E / math.sqrt(HEAD_DIM), 1.0)
    w_all = (w_in[0] * col_scale).astype(BF16)
    gain = norm_gain[0].reshape(1, d)

    m, ws, p, aa, ab = _ssm_chunk_params(
        ssm_a_re[0], ssm_a_im[0], ssm_log_dt[0], ssm_b_re[0], ssm_b_im[0], ssm_c_re[0], ssm_c_im[0])
    s, h = _ssm(x, w_all, [gain, m, ws, p, aa, ab, ssm_d[0].reshape(1, sw),
                        w_glu[0].astype(BF16), b_glu[0].reshape(1, 2 * sw)], steps=SSM_STEPS)

    consts = [gate_bias[0].reshape(1, 2 * d), _band_bias(rel_bias[0]),
              w_attn_out[0].astype(BF16), w_ssm_out[0].astype(BF16), w_out[0].astype(BF16),
              final_gain.reshape(1, d)]
    return _main(x, h, s, w_all, consts)
```

```python
import functools
import math

import jax
import jax.numpy as jnp
from jax import lax
from jax.experimental import pallas as pl
from jax.experimental.pallas import tpu as pltpu

F32 = jnp.float32
BF16 = jnp.bfloat16

CHUNK = 64
LEFT_CHUNKS = 8
BAND = (LEFT_CHUNKS + 1) * CHUNK
HIST = LEFT_CHUNKS * CHUNK
N_HEADS = 8
HEAD_DIM = 64
ATTN_WIDTH = N_HEADS * HEAD_DIM
SSM_WIDTH = 512
SSM_GROUP = 16
SSM_GROUPS = SSM_WIDTH // SSM_GROUP
SSM_STATE = 64
NORM_EPS = 1e-6
NEG_INF = -1e30
LOG2E = math.log2(math.e)

V7X_VMEM_BYTES = 64 * 1024 * 1024
VMEM_LIMIT_BYTES = V7X_VMEM_BYTES - 8 * 1024 * 1024
SSM_STEPS = 128
SSM_SUB_BLOCKS = 2
MAIN_TILE = HIST
SUB_TILES = 2


def _rms_norm(x, gain):
    inv = lax.rsqrt(jnp.mean(x * x, axis=-1, keepdims=True) + NORM_EPS)
    return x * inv * gain


def _const_spec(shape):
    zeros = (0,) * len(shape)
    return pl.BlockSpec(shape, lambda *_: zeros, pipeline_mode=pl.Buffered(1))


def _col_block_spec(rows, width, index):
    return pl.BlockSpec((rows, width), lambda *_: (0, index), pipeline_mode=pl.Buffered(1))


LANE_TILE = 128
SSM_CHUNK = LANE_TILE // SSM_GROUP
GROUPS_PER_TILE = LANE_TILE // SSM_GROUP
STATE_LANES = 2 * SSM_STATE


def _block_transpose(arrs):
    n = len(arrs)
    lane_block = lax.broadcasted_iota(jnp.int32, arrs[0].shape, 1) // SSM_GROUP
    rot = [a if i == 0 else pltpu.roll(a, SSM_GROUP * i, axis=1) for i, a in enumerate(arrs)]
    x = [rot[-k % n] for k in range(n)]
    for d in (1, 2, 4):
        has_bit = (lane_block & d) != 0
        x = [jnp.where(has_bit, x[(k - d) % n], x[k]) for k in range(n)]
    return [a if k == 0 else pltpu.roll(a, LANE_TILE - SSM_GROUP * k, axis=1) for k, a in enumerate(x)]


def _ssm_kernel(*refs, steps, batch):
    st, st_sw = refs[-2:]

    @pl.when(pl.program_id(0) == 0)
    def _():
        st[...] = jnp.zeros_like(st)
        st_sw[...] = jnp.zeros_like(st_sw)

    for sub in range(SSM_SUB_BLOCKS):
        _ssm_block(*refs, slice(steps * sub, steps * (sub + 1)), steps=steps, batch=batch)


def _ssm_block(x_ref, gain_ref, wu_ref, m_ref, ws_ref, p_ref, aa_ref, ab_ref, d_ref, wglu_ref, bglu_ref,
               s_ref, h_ref, slab_in, slab_out, ur_s, sc_s, scw_s, hp_s, st, st_sw, tsl,
               *, steps, batch):
    n_chunks = steps // SSM_CHUNK
    rows_cb = n_chunks * batch
    pitch = steps + 8
    n_tiles = SSM_WIDTH // LANE_TILE

    x = x_ref[:, tsl, :].reshape(batch * steps, x_ref.shape[-1])
    h = _rms_norm(x, gain_ref[...]).astype(BF16)
    h_ref[:, tsl, :] = h.reshape(batch, steps, -1)
    u_bt = jnp.dot(h, wu_ref[...], preferred_element_type=F32)

    for j in range(n_tiles):
        for b in range(batch):
            slab_in[j, pitch * b:pitch * b + steps, :] = (
                u_bt[steps * b:steps * (b + 1), LANE_TILE * j:LANE_TILE * (j + 1)])
    group_lanes = GROUPS_PER_TILE * STATE_LANES
    for j in range(n_tiles):
        lanes_j = slice(LANE_TILE * j, LANE_TILE * (j + 1))
        by_step = [
            jnp.concatenate([slab_in[j, pl.ds(SSM_CHUNK * c + tau, batch, stride=pitch), :]
                             for c in range(n_chunks)], axis=0)
            for tau in range(SSM_CHUNK)]
        for tau in range(SSM_CHUNK):
            ur_s[rows_cb * tau:rows_cb * (tau + 1), lanes_j] = by_step[tau]
        by_group = _block_transpose(by_step)

        intra = []
        for g8 in range(GROUPS_PER_TILE):
            g = GROUPS_PER_TILE * j + g8
            ug = by_group[g8].astype(BF16)
            intra.append(jnp.dot(ug, m_ref[g], preferred_element_type=F32))
            state = jnp.dot(ug, ws_ref[g], preferred_element_type=F32)
            sc_s[:, STATE_LANES * g:STATE_LANES * (g + 1)] = state[:, :STATE_LANES]
            scw_s[:, STATE_LANES * g:STATE_LANES * (g + 1)] = state[:, STATE_LANES:]

        sl = slice(group_lanes * j, group_lanes * (j + 1))
        aa = jnp.broadcast_to(aa_ref[:, sl], (batch, group_lanes))
        ab = jnp.broadcast_to(ab_ref[:, sl], (batch, group_lanes))
        hh, hw = st[:, sl], st_sw[:, sl]
        for c in range(n_chunks):
            rows = slice(batch * c, batch * (c + 1))
            hp_s[rows, sl] = hh
            hh, hw = (aa * hh + ab * hw + sc_s[rows, sl], aa * hw - ab * hh + scw_s[rows, sl])
        st[:, sl] = hh
        st_sw[:, sl] = hw

        by_group = []
        for g8 in range(GROUPS_PER_TILE):
            g = GROUPS_PER_TILE * j + g8
            hp = hp_s[:, STATE_LANES * g:STATE_LANES * (g + 1)].astype(BF16)
            by_group.append(intra[g8] + jnp.dot(hp, p_ref[g], preferred_element_type=F32))
        by_step = _block_transpose(by_group)
        for tau in range(SSM_CHUNK):
            rows = slice(rows_cb * tau, rows_cb * (tau + 1))
            ur_s[rows, lanes_j] = by_step[tau] + d_ref[:, lanes_j] * ur_s[rows, lanes_j]

    g_act = jax.nn.gelu(ur_s[...])
    z = jnp.dot(g_act.astype(BF16), wglu_ref[...], preferred_element_type=F32) + bglu_ref[...]
    s = z[:, :SSM_WIDTH] * jax.nn.sigmoid(z[:, SSM_WIDTH:])

    for tau in range(SSM_CHUNK):
        for c in range(n_chunks):
            r0 = rows_cb * tau + batch * c
            for j in range(n_tiles):
                slab_out[j, pl.ds(SSM_CHUNK * c + tau, batch, stride=pitch), :] = (
                    s[r0:r0 + batch, LANE_TILE * j:LANE_TILE * (j + 1)])
    for b in range(batch):
        s_ref[b, tsl, :] = jnp.concatenate(
            [slab_out[j, pitch * b:pitch * b + steps, :] for j in range(n_tiles)], axis=1)


def _ssm(x, w_all, consts, steps):
    batch, l, d = x.shape
    n_tiles = SSM_WIDTH // LANE_TILE
    rows_cb = steps // SSM_CHUNK * batch
    span = steps * SSM_SUB_BLOCKS
    kern = functools.partial(_ssm_kernel, steps=steps, batch=batch)
    slab = pltpu.VMEM((n_tiles, (steps + 8) * batch, LANE_TILE), F32)
    state_cols = SSM_GROUPS * STATE_LANES
    return pl.pallas_call(
        kern,
        grid=(l // span,),
        in_specs=[pl.BlockSpec((batch, span, d), lambda i: (0, i, 0)), _const_spec(consts[0].shape),
                  _col_block_spec(d, SSM_WIDTH, U_COL_BLOCK)]
        + [_const_spec(c.shape) for c in consts[1:]],
        out_specs=[pl.BlockSpec((batch, span, SSM_WIDTH), lambda i: (0, i, 0)),
                   pl.BlockSpec((batch, span, d), lambda i: (0, i, 0))],
        out_shape=[jax.ShapeDtypeStruct((batch, l, SSM_WIDTH), F32),
                   jax.ShapeDtypeStruct((batch, l, d), BF16)],
        scratch_shapes=[
            slab, slab,
            pltpu.VMEM((steps * batch, SSM_WIDTH), F32),
            pltpu.VMEM((rows_cb, state_cols), F32), pltpu.VMEM((rows_cb, state_cols), F32),
            pltpu.VMEM((rows_cb, state_cols), F32),
            pltpu.VMEM((batch, state_cols), F32), pltpu.VMEM((batch, state_cols), F32),
        ],
        compiler_params=pltpu.CompilerParams(
            dimension_semantics=("arbitrary",), vmem_limit_bytes=VMEM_LIMIT_BYTES),
        name="ssm",
    )(x, consts[0], w_all, *consts[1:])


PAIR = 2 * CHUNK
WINDOW = BAND + CHUNK
N_PAIRS = N_HEADS // 2
U_COL_BLOCK = 4
MAIN_W_IN_BLOCKS = ((ATTN_WIDTH, 0), (ATTN_WIDTH, 1), (ATTN_WIDTH, 2), (ATTN_WIDTH, 3),
                    (SSM_WIDTH, 5), (2 * ATTN_WIDTH, 3), (2 * ATTN_WIDTH, 4))


class _AttnRefs:
    def __init__(self, bias_ref, qm_s, k_s, vt_s, yat_s, s_bufs, p_bufs, r_bufs):
        self.bias_ref, self.qm_s, self.k_s, self.vt_s, self.yat_s = bias_ref, qm_s, k_s, vt_s, yat_s
        self.s_bufs, self.p_bufs, self.r_bufs = s_bufs, p_bufs, r_bufs
        self.tm = MAIN_TILE


def _unit_index(k):
    if isinstance(k, int):
        return k // N_PAIRS, k % N_PAIRS
    return lax.shift_right_logical(k, 2), lax.bitwise_and(k, N_PAIRS - 1)


def _row_start(idx, size):
    return idx * size if isinstance(idx, int) else pl.multiple_of(idx * size, size)


def _nt_dot(a, b):
    return lax.dot_general(a, b, (((1,), (1,)), ((), ())), preferred_element_type=F32)


def _scores_stage(r, k, slot, masked_start, sub):
    m, pr = _unit_index(k)
    kwin = r.k_s[pr, pl.ds(r.tm * sub + _row_start(m, PAIR), WINDOW), :]
    s = _nt_dot(kwin, r.qm_s[sub, pr, m]) + r.bias_ref[pr]
    if not (isinstance(masked_start, int) and masked_start == 0):
        row = lax.broadcasted_iota(jnp.int32, s.shape, 0)
        s = jnp.where(row >= masked_start * (HIST - PAIR * m), s, NEG_INF)
    r.s_bufs[slot][...] = s


def _softmax_stage(r, slot):
    s = r.s_bufs[slot][...]
    p = jnp.exp2(s - jnp.max(s, axis=0, keepdims=True))
    r.r_bufs[slot][...] = 1.0 / jnp.sum(p, axis=0, keepdims=True)
    r.p_bufs[slot][...] = p.astype(BF16)


def _values_stage(r, k, slot, sub):
    m, pr = _unit_index(k)
    vwin = r.vt_s[pl.ds(_row_start(pr, PAIR), PAIR), pl.ds(r.tm * sub + _row_start(m, PAIR), WINDOW)]
    o = jnp.dot(vwin, r.p_bufs[slot][...], preferred_element_type=F32) * r.r_bufs[slot][...]
    r.yat_s[sub, m, pl.ds(_row_start(pr, PAIR), PAIR), :] = jnp.concatenate(
        [o[:HEAD_DIM, :PAIR], o[HEAD_DIM:, PAIR:]], axis=0)


def _attention(r, n_units, masked_start, sub):
    for k in range(n_units + 2):
        if k < n_units:
            _scores_stage(r, k, k % 2, masked_start, sub)
        if 1 <= k <= n_units:
            _softmax_stage(r, (k - 1) % 2)
        if k >= 2:
            _values_stage(r, k - 2, (k - 2) % 2, sub)


def _main_kernel(x_ref, h_ref, s_ref, wq_ref, wk_ref, wv_ref, wza_ref, wzs_ref, wga_ref, wgs_ref,
                 gb_ref, band_bias_ref, wao_ref, wso_ref, wo_ref, fgain_ref, o_ref,
                 k_s, vt_s, qm_s, yat_s, s_buf0, s_buf1, p_buf0, p_buf1, r_buf0, r_buf1,
                 bias_s, *, tm):
    i = pl.program_id(1)
    n_m = tm // PAIR

    @pl.when((pl.program_id(0) == 0) & (i == 0))
    def _():
        masked = jnp.full((CHUNK, CHUNK), NEG_INF, F32)
        for pr in range(N_PAIRS):
            blocks = []
            for head in (2 * pr, 2 * pr + 1):
                band = band_bias_ref[head]
                blocks.append(jnp.concatenate([band, masked], axis=1))
                blocks.append(jnp.concatenate([masked, band], axis=1))
            bias_s[pr] = jnp.concatenate(blocks, axis=0).T
        k_s[:, 0:HIST, :] = jnp.zeros((N_PAIRS, HIST, PAIR), BF16)
        vt_s[:, 0:HIST] = jnp.zeros((ATTN_WIDTH, HIST), BF16)

    refs = _AttnRefs(bias_s, qm_s, k_s, vt_s, yat_s,
                     (s_buf0, s_buf1), (p_buf0, p_buf1), (r_buf0, r_buf1))
    d = x_ref.shape[-1]
    lane = lax.broadcasted_iota(jnp.int32, (tm, PAIR), 1)
    for sub in range(SUB_TILES):
        rows = slice(tm * sub, tm * (sub + 1))
        pos = slice(HIST + tm * sub, HIST + tm * (sub + 1))
        h = h_ref[rows, :]

        q = jnp.dot(h, wq_ref[...], preferred_element_type=F32)
        for pr in range(N_PAIRS):
            qp = q[:, PAIR * pr:PAIR * (pr + 1)]
            lo = jnp.where(lane < HEAD_DIM, qp, 0.0).astype(BF16)
            hi = jnp.where(lane >= HEAD_DIM, qp, 0.0).astype(BF16)
            for m in range(n_m):
                qm_s[sub, pr, m, 0:PAIR, :] = lo[PAIR * m:PAIR * (m + 1)]
                qm_s[sub, pr, m, PAIR:2 * PAIR, :] = hi[PAIR * m:PAIR * (m + 1)]

        k_new = jnp.dot(h, wk_ref[...], preferred_element_type=F32)
        vt_new = jnp.dot(h, wv_ref[...], preferred_element_type=F32).T
        vt_s[:, pos] = vt_new.astype(BF16)
        for pr in range(N_PAIRS):
            k_s[pr, pos, :] = k_new[:, PAIR * pr:PAIR * (pr + 1)].astype(BF16)

        masked_start = (i == 0).astype(jnp.int32) if sub == 0 else 0
        _attention(refs, n_m * N_PAIRS, masked_start, sub)

        za = jnp.dot(h, wza_ref[...], preferred_element_type=F32)
        yat = jnp.concatenate([yat_s[sub, m] for m in range(n_m)], axis=1)
        ya = yat.T * jax.nn.silu(za)
        yao = jnp.dot(ya.astype(BF16), wao_ref[...], preferred_element_type=F32)
        zs = jnp.dot(h, wzs_ref[...], preferred_element_type=F32)
        ys = s_ref[rows, :] * jax.nn.silu(zs)
        yso = jnp.dot(ys.astype(BF16), wso_ref[...], preferred_element_type=F32)
        ga = jnp.dot(h, wga_ref[...], preferred_element_type=F32) + gb_ref[:, :d]
        gs = jnp.dot(h, wgs_ref[...], preferred_element_type=F32) + gb_ref[:, d:]
        merged = jax.nn.sigmoid(ga) * yao + jax.nn.sigmoid(gs) * yso
        xn = x_ref[rows, :] + jnp.dot(merged.astype(BF16), wo_ref[...], preferred_element_type=F32)
        o_ref[rows, :] = _rms_norm(xn, fgain_ref[...])
    span = SUB_TILES * tm
    k_s[:, 0:HIST, :] = k_s[:, span:span + HIST, :]
    vt_s[:, 0:HIST] = vt_s[:, span:span + HIST]


def _main(x, h, s, w_all, consts):
    b, l, d = x.shape
    tm = MAIN_TILE
    n_m = tm // PAIR
    kern = functools.partial(_main_kernel, tm=tm)
    unit_rows = 2 * PAIR
    span = SUB_TILES * tm
    return pl.pallas_call(
        kern,
        grid=(b, l // span),
        in_specs=[
            pl.BlockSpec((None, span, d), lambda bi, i: (bi, i, 0)),
            pl.BlockSpec((None, span, d), lambda bi, i: (bi, i, 0)),
            pl.BlockSpec((None, span, SSM_WIDTH), lambda bi, i: (bi, i, 0)),
        ] + [_col_block_spec(d, width, index) for width, index in MAIN_W_IN_BLOCKS]
        + [_const_spec(c.shape) for c in consts],
        out_specs=pl.BlockSpec((None, span, d), lambda bi, i: (bi, i, 0)),
        out_shape=jax.ShapeDtypeStruct((b, l, d), F32),
        scratch_shapes=[
            pltpu.VMEM((N_PAIRS, HIST + span, PAIR), BF16),
            pltpu.VMEM((ATTN_WIDTH, HIST + span), BF16),
            pltpu.VMEM((SUB_TILES, N_PAIRS, n_m, unit_rows, PAIR), BF16),
            pltpu.VMEM((SUB_TILES, n_m, ATTN_WIDTH, PAIR), F32),
            pltpu.VMEM((WINDOW, unit_rows), F32), pltpu.VMEM((WINDOW, unit_rows), F32),
            pltpu.VMEM((WINDOW, unit_rows), BF16), pltpu.VMEM((WINDOW, unit_rows), BF16),
            pltpu.VMEM((1, unit_rows), F32), pltpu.VMEM((1, unit_rows), F32),
            pltpu.VMEM((N_PAIRS, WINDOW, unit_rows), F32),
        ],
        compiler_params=pltpu.CompilerParams(
            dimension_semantics=("arbitrary", "arbitrary"), vmem_limit_bytes=VMEM_LIMIT_BYTES),
        name="main_block",
    )(x, h, s, *([w_all] * len(MAIN_W_IN_BLOCKS)), *consts)


def _ssm_chunk_params(a_re, a_im, log_dt, b_re, b_im, c_re, c_im):
    hi = lax.Precision.HIGHEST
    n = SSM_CHUNK
    dt = jnp.exp(log_dt)[:, None]
    k = jnp.arange(n + 1, dtype=F32)[None, :, None]
    mag = jnp.exp((a_re * dt)[:, None, :] * k)
    pw_re = mag * jnp.cos((a_im * dt)[:, None, :] * k)
    pw_im = mag * jnp.sin((a_im * dt)[:, None, :] * k)
    n_re = pw_re[:, 1] - 1.0
    n_im = pw_im[:, 1]
    den = a_re * a_re + a_im * a_im
    f_re = ((n_re * a_re + n_im * a_im) / den)[:, None, :]
    f_im = ((n_im * a_re - n_re * a_im) / den)[:, None, :]
    bt_re, bt_im = jnp.swapaxes(b_re, 1, 2), jnp.swapaxes(b_im, 1, 2)
    bb_re = f_re * bt_re - f_im * bt_im
    bb_im = f_re * bt_im + f_im * bt_re
    w_re = pw_re[:, :n, None, :] * bb_re[:, None] - pw_im[:, :n, None, :] * bb_im[:, None]
    w_im = pw_re[:, :n, None, :] * bb_im[:, None] + pw_im[:, :n, None, :] * bb_re[:, None]
    kern = (jnp.einsum("gop,gkcp->gcko", c_re, w_re, precision=hi)
            - jnp.einsum("gop,gkcp->gcko", c_im, w_im, precision=hi))
    kern = kern.reshape(SSM_GROUPS, SSM_GROUP, LANE_TILE)
    zeros = jnp.zeros_like(kern)
    m = jnp.stack([jnp.concatenate([zeros[:, :, :SSM_GROUP * sig], kern[:, :, :SSM_GROUP * (n - sig)]],
                                   axis=2) for sig in range(n)], axis=1)
    m = m.reshape(SSM_GROUPS, LANE_TILE, LANE_TILE).astype(BF16)
    ws_re = w_re[:, ::-1].reshape(SSM_GROUPS, LANE_TILE, SSM_STATE)
    ws_im = w_im[:, ::-1].reshape(SSM_GROUPS, LANE_TILE, SSM_STATE)
    ws = jnp.concatenate([ws_re, ws_im, ws_im, ws_re], axis=2).astype(BF16)
    ct_re, ct_im = jnp.swapaxes(c_re, 1, 2)[:, :, None, :], jnp.swapaxes(c_im, 1, 2)[:, :, None, :]
    pt_re = jnp.swapaxes(pw_re[:, 1:], 1, 2)[..., None]
    pt_im = jnp.swapaxes(pw_im[:, 1:], 1, 2)[..., None]
    p_re = (ct_re * pt_re - ct_im * pt_im).reshape(SSM_GROUPS, SSM_STATE, LANE_TILE)
    p_im = (ct_re * pt_im + ct_im * pt_re).reshape(SSM_GROUPS, SSM_STATE, LANE_TILE)
    p = jnp.concatenate([p_re, -p_im], axis=1).astype(BF16)
    aa = jnp.concatenate([pw_re[:, n], pw_re[:, n]], axis=1).reshape(1, -1)
    ab = jnp.concatenate([-pw_im[:, n], pw_im[:, n]], axis=1).reshape(1, -1)
    return m, ws, p, aa, ab


def _band_bias(rel_bias):
    n_rel = rel_bias.shape[1]
    ext = jnp.concatenate(
        [rel_bias, jnp.broadcast_to(rel_bias[:, -1:], (N_HEADS, CHUNK + BAND - 1 - n_rel))], axis=1)
    by_rev_k = jnp.stack([ext[:, q:q + BAND] for q in range(CHUNK)], axis=1)
    return by_rev_k[:, :, ::-1] * LOG2E


def kernel(x, norm_gain, w_in, rel_bias, ssm_a_re, ssm_a_im, ssm_log_dt, ssm_b_re, ssm_b_im,
           ssm_c_re, ssm_c_im, ssm_d, w_glu, b_glu, w_attn_out, w_ssm_out, gate_bias, w_out,
           final_gain):
    assert norm_gain.shape[0] == 1, "single-layer block"
    b, l, d = x.shape
    aw, sw = ATTN_WIDTH, SSM_WIDTH
    col_scale = jnp.where(jnp.arange(w_in.shape[-1]) < aw, LOG2E / math.sqrt(HEAD_DIM), 1.0)
    w_all = (w_in[0] * col_scale).astype(BF16)
    gain = norm_gain[0].reshape(1, d)

    m, ws, p, aa, ab = _ssm_chunk_params(
        ssm_a_re[0], ssm_a_im[0], ssm_log_dt[0], ssm_b_re[0], ssm_b_im[0], ssm_c_re[0], ssm_c_im[0])
    s, h = _ssm(x, w_all, [gain, m, ws, p, aa, ab, ssm_d[0].reshape(1, sw),
                        w_glu[0].astype(BF16), b_glu[0].reshape(1, 2 * sw)], steps=SSM_STEPS)

    consts = [gate_bias[0].reshape(1, 2 * d), _band_bias(rel_bias[0]),
              w_attn_out[0].astype(BF16), w_ssm_out[0].astype(BF16), w_out[0].astype(BF16),
              final_gain.reshape(1, d)]
    return _main(x, h, s, w_all, consts)
```

```python
import functools
import math

import jax
import jax.numpy as jnp
from jax import lax
from jax.experimental import pallas as pl
from jax.experimental.pallas import tpu as pltpu

F32 = jnp.float32
BF16 = jnp.bfloat16

CHUNK = 64
LEFT_CHUNKS = 8
BAND = (LEFT_CHUNKS + 1) * CHUNK
HIST = LEFT_CHUNKS * CHUNK
N_HEADS = 8
HEAD_DIM = 64
ATTN_WIDTH = N_HEADS * HEAD_DIM
SSM_WIDTH = 512
SSM_GROUP = 16
SSM_GROUPS = SSM_WIDTH // SSM_GROUP
SSM_STATE = 64
NORM_EPS = 1e-6
NEG_INF = -1e30
LOG2E = math.log2(math.e)

V7X_VMEM_BYTES = 64 * 1024 * 1024
VMEM_LIMIT_BYTES = V7X_VMEM_BYTES - 8 * 1024 * 1024
SSM_STEPS = 128
SSM_SUB_BLOCKS = 2
MAIN_TILE = HIST
SUB_TILES = 2


def _rms_norm(x, gain):
    inv = lax.rsqrt(jnp.mean(x * x, axis=-1, keepdims=True) + NORM_EPS)
    return x * inv * gain


def _const_spec(shape):
    zeros = (0,) * len(shape)
    return pl.BlockSpec(shape, lambda *_: zeros, pipeline_mode=pl.Buffered(1))


def _col_block_spec(rows, width, index):
    return pl.BlockSpec((rows, width), lambda *_: (0, index), pipeline_mode=pl.Buffered(1))


LANE_TILE = 128
SSM_CHUNK = LANE_TILE // SSM_GROUP
GROUPS_PER_TILE = LANE_TILE // SSM_GROUP
STATE_LANES = 2 * SSM_STATE


def _block_transpose(arrs):
    n = len(arrs)
    lane_block = lax.broadcasted_iota(jnp.int32, arrs[0].shape, 1) // SSM_GROUP
    rot = [a if i == 0 else pltpu.roll(a, SSM_GROUP * i, axis=1) for i, a in enumerate(arrs)]
    x = [rot[-k % n] for k in range(n)]
    for d in (1, 2, 4):
        has_bit = (lane_block & d) != 0
        x = [jnp.where(has_bit, x[(k - d) % n], x[k]) for k in range(n)]
    return [a if k == 0 else pltpu.roll(a, LANE_TILE - SSM_GROUP * k, axis=1) for k, a in enumerate(x)]


def _ssm_kernel(*refs, steps, batch):
    st, st_sw = refs[-2:]

    @pl.when(pl.program_id(0) == 0)
    def _():
        st[...] = jnp.zeros_like(st)
        st_sw[...] = jnp.zeros_like(st_sw)

    for sub in range(SSM_SUB_BLOCKS):
        _ssm_block(*refs, slice(steps * sub, steps * (sub + 1)), steps=steps, batch=batch)


def _ssm_block(x_ref, gain_ref, wu_ref, m_ref, ws_ref, p_ref, aa_ref, ab_ref, d_ref, wglu_ref, bglu_ref,
               s_ref, h_ref, slab_in, slab_out, ur_s, sc_s, scw_s, hp_s, st, st_sw, tsl,
               *, steps, batch):
    n_chunks = steps // SSM_CHUNK
    rows_cb = n_chunks * batch
    pitch = steps + 8
    n_tiles = SSM_WIDTH // LANE_TILE

    x = x_ref[:, tsl, :].reshape(batch * steps, x_ref.shape[-1])
    h = _rms_norm(x, gain_ref[...]).astype(BF16)
    h_ref[:, tsl, :] = h.reshape(batch, steps, -1)
    u_bt = jnp.dot(h, wu_ref[...], preferred_element_type=F32)

    for j in range(n_tiles):
        for b in range(batch):
            slab_in[j, pitch * b:pitch * b + steps, :] = (
                u_bt[steps * b:steps * (b + 1), LANE_TILE * j:LANE_TILE * (j + 1)])
    group_lanes = GROUPS_PER_TILE * STATE_LANES
    for j in range(n_tiles):
        lanes_j = slice(LANE_TILE * j, LANE_TILE * (j + 1))
        by_step = [
            jnp.concatenate([slab_in[j, pl.ds(SSM_CHUNK * c + tau, batch, stride=pitch), :]
                             for c in range(n_chunks)], axis=0)
            for tau in range(SSM_CHUNK)]
        for tau in range(SSM_CHUNK):
            ur_s[rows_cb * tau:rows_cb * (tau + 1), lanes_j] = by_step[tau]
        by_group = _block_transpose(by_step)

        intra = []
        for g8 in range(GROUPS_PER_TILE):
            g = GROUPS_PER_TILE * j + g8
            ug = by_group[g8].astype(BF16)
            intra.append(jnp.dot(ug, m_ref[g], preferred_element_type=F32))
            state = jnp.dot(ug, ws_ref[g], preferred_element_type=F32)
            sc_s[:, STATE_LANES * g:STATE_LANES * (g + 1)] = state[:, :STATE_LANES]
            scw_s[:, STATE_LANES * g:STATE_LANES * (g + 1)] = state[:, STATE_LANES:]

        sl = slice(group_lanes * j, group_lanes * (j + 1))
        aa = jnp.broadcast_to(aa_ref[:, sl], (batch, group_lanes))
        ab = jnp.broadcast_to(ab_ref[:, sl], (batch, group_lanes))
        hh, hw = st[:, sl], st_sw[:, sl]
        for c in range(n_chunks):
            rows = slice(batch * c, batch * (c + 1))
            hp_s[rows, sl] = hh
            hh, hw = (aa * hh + ab * hw + sc_s[rows, sl], aa * hw - ab * hh + scw_s[rows, sl])
        st[:, sl] = hh
        st_sw[:, sl] = hw

        by_group = []
        for g8 in range(GROUPS_PER_TILE):
            g = GROUPS_PER_TILE * j + g8
            hp = hp_s[:, STATE_LANES * g:STATE_LANES * (g + 1)].astype(BF16)
            by_group.append(intra[g8] + jnp.dot(hp, p_ref[g], preferred_element_type=F32))
        by_step = _block_transpose(by_group)
        for tau in range(SSM_CHUNK):
            rows = slice(rows_cb * tau, rows_cb * (tau + 1))
            ur_s[rows, lanes_j] = by_step[tau] + d_ref[:, lanes_j] * ur_s[rows, lanes_j]

    g_act = jax.nn.gelu(ur_s[...])
    z = jnp.dot(g_act.astype(BF16), wglu_ref[...], preferred_element_type=F32) + bglu_ref[...]
    s = z[:, :SSM_WIDTH] * jax.nn.sigmoid(z[:, SSM_WIDTH:])

    for tau in range(SSM_CHUNK):
        for c in range(n_chunks):
            r0 = rows_cb * tau + batch * c
            for j in range(n_tiles):
                slab_out[j, pl.ds(SSM_CHUNK * c + tau, batch, stride=pitch), :] = (
                    s[r0:r0 + batch, LANE_TILE * j:LANE_TILE * (j + 1)])
    for b in range(batch):
        s_ref[b, tsl, :] = jnp.concatenate(
            [slab_out[j, pitch * b:pitch * b + steps, :] for j in range(n_tiles)], axis=1)


def _ssm(x, w_all, consts, steps):
    batch, l, d = x.shape
    n_tiles = SSM_WIDTH // LANE_TILE
    rows_cb = steps // SSM_CHUNK * batch
    span = steps * SSM_SUB_BLOCKS
    kern = functools.partial(_ssm_kernel, steps=steps, batch=batch)
    slab = pltpu.VMEM((n_tiles, (steps + 8) * batch, LANE_TILE), F32)
    state_cols = SSM_GROUPS * STATE_LANES
    return pl.pallas_call(
        kern,
        grid=(l // span,),
        in_specs=[pl.BlockSpec((batch, span, d), lambda i: (0, i, 0)), _const_spec(consts[0].shape),
                  _col_block_spec(d, SSM_WIDTH, U_COL_BLOCK)]
        + [_const_spec(c.shape) for c in consts[1:]],
        out_specs=[pl.BlockSpec((batch, span, SSM_WIDTH), lambda i: (0, i, 0)),
                   pl.BlockSpec((batch, span, d), lambda i: (0, i, 0))],
        out_shape=[jax.ShapeDtypeStruct((batch, l, SSM_WIDTH), F32),
                   jax.ShapeDtypeStruct((batch, l, d), BF16)],
        scratch_shapes=[
            slab, slab,
            pltpu.VMEM((steps * batch, SSM_WIDTH), F32),
            pltpu.VMEM((rows_cb, state_cols), F32), pltpu.VMEM((rows_cb, state_cols), F32),
            pltpu.VMEM((rows_cb, state_cols), F32),
            pltpu.VMEM((batch, state_cols), F32), pltpu.VMEM((batch, state_cols), F32),
        ],
        compiler_params=pltpu.CompilerParams(
            dimension_semantics=("arbitrary",), vmem_limit_bytes=VMEM_LIMIT_BYTES),
        name="ssm",
    )(x, consts[0], w_all, *consts[1:])


PAIR = 2 * CHUNK
WINDOW = BAND + CHUNK
N_PAIRS = N_HEADS // 2
U_COL_BLOCK = 4
MAIN_W_IN_BLOCKS = ((ATTN_WIDTH, 0), (ATTN_WIDTH, 1), (ATTN_WIDTH, 2), (ATTN_WIDTH, 3),
                    (SSM_WIDTH, 5), (2 * ATTN_WIDTH, 3), (2 * ATTN_WIDTH, 4))


class _AttnRefs:
    def __init__(self, bias_ref, qm_s, k_s, vt_s, yat_s, s_bufs, p_bufs, r_bufs):
        self.bias_ref, self.qm_s, self.k_s, self.vt_s, self.yat_s = bias_ref, qm_s, k_s, vt_s, yat_s
        self.s_bufs, self.p_bufs, self.r_bufs = s_bufs, p_bufs, r_bufs
        self.tm = MAIN_TILE


def _unit_index(k):
    if isinstance(k, int):
        return k // N_PAIRS, k % N_PAIRS
    return lax.shift_right_logical(k, 2), lax.bitwise_and(k, N_PAIRS - 1)


def _row_start(idx, size):
    return idx * size if isinstance(idx, int) else pl.multiple_of(idx * size, size)


def _nt_dot(a, b):
    return lax.dot_general(a, b, (((1,), (1,)), ((), ())), preferred_element_type=F32)


def _scores_stage(r, k, slot, masked_start, sub):
    m, pr = _unit_index(k)
    kwin = r.k_s[pr, pl.ds(r.tm * sub + _row_start(m, PAIR), WINDOW), :]
    s = _nt_dot(kwin, r.qm_s[sub, pr, m]) + r.bias_ref[pr]
    if not (isinstance(masked_start, int) and masked_start == 0):
        row = lax.broadcasted_iota(jnp.int32, s.shape, 0)
        s = jnp.where(row >= masked_start * (HIST - PAIR * m), s, NEG_INF)
    r.s_bufs[slot][...] = s


def _softmax_stage(r, slot):
    s = r.s_bufs[slot][...]
    p = jnp.exp2(s - jnp.max(s, axis=0, keepdims=True))
    r.r_bufs[slot][...] = 1.0 / jnp.sum(p, axis=0, keepdims=True)
    r.p_bufs[slot][...] = p.astype(BF16)


def _values_stage(r, k, slot, sub):
    m, pr = _unit_index(k)
    vwin = r.vt_s[pl.ds(_row_start(pr, PAIR), PAIR), pl.ds(r.tm * sub + _row_start(m, PAIR), WINDOW)]
    o = jnp.dot(vwin, r.p_bufs[slot][...], preferred_element_type=F32) * r.r_bufs[slot][...]
    r.yat_s[sub, m, pl.ds(_row_start(pr, PAIR), PAIR), :] = jnp.concatenate(
        [o[:HEAD_DIM, :PAIR], o[HEAD_DIM:, PAIR:]], axis=0)


def _attention(r, n_units, masked_start, sub):
    for k in range(n_units + 2):
        if k < n_units:
            _scores_stage(r, k, k % 2, masked_start, sub)
        if 1 <= k <= n_units:
            _softmax_stage(r, (k - 1) % 2)
        if k >= 2:
            _values_stage(r, k - 2, (k - 2) % 2, sub)


def _main_kernel(x_ref, h_ref, s_ref, w_hbm,
                 gb_ref, band_bias_ref, wao_ref, wso_ref, wo_ref, fgain_ref, o_ref,
                 k_s, vt_s, qm_s, yat_s, s_buf0, s_buf1, p_buf0, p_buf1, r_buf0, r_buf1,
                 bias_s, w_s, w_sem, *, tm):
    i = pl.program_id(1)
    n_m = tm // PAIR
    wq, wk, wv, wza, wzs, wga, wgs = (
        slice(width * index, width * (index + 1)) for width, index in MAIN_W_IN_BLOCKS)

    @pl.when((pl.program_id(0) == 0) & (i == 0))
    def _():
        w_copy = pltpu.make_async_copy(w_hbm, w_s, w_sem)
        w_copy.start()
        masked = jnp.full((CHUNK, CHUNK), NEG_INF, F32)
        for pr in range(N_PAIRS):
            blocks = []
            for head in (2 * pr, 2 * pr + 1):
                band = band_bias_ref[head]
                blocks.append(jnp.concatenate([band, masked], axis=1))
                blocks.append(jnp.concatenate([masked, band], axis=1))
            bias_s[pr] = jnp.concatenate(blocks, axis=0).T
        k_s[:, 0:HIST, :] = jnp.zeros((N_PAIRS, HIST, PAIR), BF16)
        vt_s[:, 0:HIST] = jnp.zeros((ATTN_WIDTH, HIST), BF16)
        w_copy.wait()

    refs = _AttnRefs(bias_s, qm_s, k_s, vt_s, yat_s,
                     (s_buf0, s_buf1), (p_buf0, p_buf1), (r_buf0, r_buf1))
    d = x_ref.shape[-1]
    lane = lax.broadcasted_iota(jnp.int32, (tm, PAIR), 1)
    for sub in range(SUB_TILES):
        rows = slice(tm * sub, tm * (sub + 1))
        pos = slice(HIST + tm * sub, HIST + tm * (sub + 1))
        h = h_ref[rows, :]

        q = jnp.dot(h, w_s[:, wq],preferred_element_type=F32)
        for pr in range(N_PAIRS):
            qp = q[:, PAIR * pr:PAIR * (pr + 1)]
            lo = jnp.where(lane < HEAD_DIM, qp, 0.0).astype(BF16)
            hi = jnp.where(lane >= HEAD_DIM, qp, 0.0).astype(BF16)
            for m in range(n_m):
                qm_s[sub, pr, m, 0:PAIR, :] = lo[PAIR * m:PAIR * (m + 1)]
                qm_s[sub, pr, m, PAIR:2 * PAIR, :] = hi[PAIR * m:PAIR * (m + 1)]

        k_new = jnp.dot(h, w_s[:, wk],preferred_element_type=F32)
        vt_new = jnp.dot(h, w_s[:, wv],preferred_element_type=F32).T
        vt_s[:, pos] = vt_new.astype(BF16)
        for pr in range(N_PAIRS):
            k_s[pr, pos, :] = k_new[:, PAIR * pr:PAIR * (pr + 1)].astype(BF16)

        masked_start = (i == 0).astype(jnp.int32) if sub == 0 else 0
        _attention(refs, n_m * N_PAIRS, masked_start, sub)

        za = jnp.dot(h, w_s[:, wza],preferred_element_type=F32)
        yat = jnp.concatenate([yat_s[sub, m] for m in range(n_m)], axis=1)
        ya = yat.T * jax.nn.silu(za)
        yao = jnp.dot(ya.astype(BF16), wao_ref[...], preferred_element_type=F32)
        zs = jnp.dot(h, w_s[:, wzs],preferred_element_type=F32)
        ys = s_ref[rows, :] * jax.nn.silu(zs)
        yso = jnp.dot(ys.astype(BF16), wso_ref[...], preferred_element_type=F32)
        ga = jnp.dot(h, w_s[:, wga],preferred_element_type=F32) + gb_ref[:, :d]
        gs = jnp.dot(h, w_s[:, wgs],preferred_element_type=F32) + gb_ref[:, d:]
        merged = jax.nn.sigmoid(ga) * yao + jax.nn.sigmoid(gs) * yso
        xn = x_ref[rows, :] + jnp.dot(merged.astype(BF16), wo_ref[...], preferred_element_type=F32)
        o_ref[rows, :] = _rms_norm(xn, fgain_ref[...])
    span = SUB_TILES * tm
    k_s[:, 0:HIST, :] = k_s[:, span:span + HIST, :]
    vt_s[:, 0:HIST] = vt_s[:, span:span + HIST]


def _main(x, h, s, w_all, consts):
    b, l, d = x.shape
    tm = MAIN_TILE
    n_m = tm // PAIR
    kern = functools.partial(_main_kernel, tm=tm)
    unit_rows = 2 * PAIR
    span = SUB_TILES * tm
    return pl.pallas_call(
        kern,
        grid=(b, l // span),
        in_specs=[
            pl.BlockSpec((None, span, d), lambda bi, i: (bi, i, 0)),
            pl.BlockSpec((None, span, d), lambda bi, i: (bi, i, 0)),
            pl.BlockSpec((None, span, SSM_WIDTH), lambda bi, i: (bi, i, 0)),
            pl.BlockSpec(memory_space=pl.ANY),
        ] + [_const_spec(c.shape) for c in consts],
        out_specs=pl.BlockSpec((None, span, d), lambda bi, i: (bi, i, 0)),
        out_shape=jax.ShapeDtypeStruct((b, l, d), F32),
        scratch_shapes=[
            pltpu.VMEM((N_PAIRS, HIST + span, PAIR), BF16),
            pltpu.VMEM((ATTN_WIDTH, HIST + span), BF16),
            pltpu.VMEM((SUB_TILES, N_PAIRS, n_m, unit_rows, PAIR), BF16),
            pltpu.VMEM((SUB_TILES, n_m, ATTN_WIDTH, PAIR), F32),
            pltpu.VMEM((WINDOW, unit_rows), F32), pltpu.VMEM((WINDOW, unit_rows), F32),
            pltpu.VMEM((WINDOW, unit_rows), BF16), pltpu.VMEM((WINDOW, unit_rows), BF16),
            pltpu.VMEM((1, unit_rows), F32), pltpu.VMEM((1, unit_rows), F32),
            pltpu.VMEM((N_PAIRS, WINDOW, unit_rows), F32),
            pltpu.VMEM(w_all.shape, w_all.dtype), pltpu.SemaphoreType.DMA(()),
        ],
        compiler_params=pltpu.CompilerParams(
            dimension_semantics=("arbitrary", "arbitrary"), vmem_limit_bytes=VMEM_LIMIT_BYTES),
        name="main_block",
    )(x, h, s, w_all, *consts)


def _ssm_chunk_params(a_re, a_im, log_dt, b_re, b_im, c_re, c_im):
    hi = lax.Precision.HIGHEST
    n = SSM_CHUNK
    dt = jnp.exp(log_dt)[:, None]
    k = jnp.arange(n + 1, dtype=F32)[None, :, None]
    mag = jnp.exp((a_re * dt)[:, None, :] * k)
    pw_re = mag * jnp.cos((a_im * dt)[:, None, :] * k)
    pw_im = mag * jnp.sin((a_im * dt)[:, None, :] * k)
    n_re = pw_re[:, 1] - 1.0
    n_im = pw_im[:, 1]
    den = a_re * a_re + a_im * a_im
    f_re = ((n_re * a_re + n_im * a_im) / den)[:, None, :]
    f_im = ((n_im * a_re - n_re * a_im) / den)[:, None, :]
    bt_re, bt_im = jnp.swapaxes(b_re, 1, 2), jnp.swapaxes(b_im, 1, 2)
    bb_re = f_re * bt_re - f_im * bt_im
    bb_im = f_re * bt_im + f_im * bt_re
    w_re = pw_re[:, :n, None, :] * bb_re[:, None] - pw_im[:, :n, None, :] * bb_im[:, None]
    w_im = pw_re[:, :n, None, :] * bb_im[:, None] + pw_im[:, :n, None, :] * bb_re[:, None]
    kern = (jnp.einsum("gop,gkcp->gcko", c_re, w_re, precision=hi)
            - jnp.einsum("gop,gkcp->gcko", c_im, w_im, precision=hi))
    kern = kern.reshape(SSM_GROUPS, SSM_GROUP, LANE_TILE)
    zeros = jnp.zeros_like(kern)
    m = jnp.stack([jnp.concatenate([zeros[:, :, :SSM_GROUP * sig], kern[:, :, :SSM_GROUP * (n - sig)]],
                                   axis=2) for sig in range(n)], axis=1)
    m = m.reshape(SSM_GROUPS, LANE_TILE, LANE_TILE).astype(BF16)
    ws_re = w_re[:, ::-1].reshape(SSM_GROUPS, LANE_TILE, SSM_STATE)
    ws_im = w_im[:, ::-1].reshape(SSM_GROUPS, LANE_TILE, SSM_STATE)
    ws = jnp.concatenate([ws_re, ws_im, ws_im, ws_re], axis=2).astype(BF16)
    ct_re, ct_im = jnp.swapaxes(c_re, 1, 2)[:, :, None, :], jnp.swapaxes(c_im, 1, 2)[:, :, None, :]
    pt_re = jnp.swapaxes(pw_re[:, 1:], 1, 2)[..., None]
    pt_im = jnp.swapaxes(pw_im[:, 1:], 1, 2)[..., None]
    p_re = (ct_re * pt_re - ct_im * pt_im).reshape(SSM_GROUPS, SSM_STATE, LANE_TILE)
    p_im = (ct_re * pt_im + ct_im * pt_re).reshape(SSM_GROUPS, SSM_STATE, LANE_TILE)
    p = jnp.concatenate([p_re, -p_im], axis=1).astype(BF16)
    aa = jnp.concatenate([pw_re[:, n], pw_re[:, n]], axis=1).reshape(1, -1)
    ab = jnp.concatenate([-pw_im[:, n], pw_im[:, n]], axis=1).reshape(1, -1)
    return m, ws, p, aa, ab


def _band_bias(rel_bias):
    n_rel = rel_bias.shape[1]
    ext = jnp.concatenate(
        [rel_bias, jnp.broadcast_to(rel_bias[:, -1:], (N_HEADS, CHUNK + BAND - 1 - n_rel))], axis=1)
    by_rev_k = jnp.stack([ext[:, q:q + BAND] for q in range(CHUNK)], axis=1)
    return by_rev_k[:, :, ::-1] * LOG2E


def kernel(x, norm_gain, w_in, rel_bias, ssm_a_re, ssm_a_im, ssm_log_dt, ssm_b_re, ssm_b_im,
           ssm_c_re, ssm_c_im, ssm_d, w_glu, b_glu, w_attn_out, w_ssm_out, gate_bias, w_out,
           final_gain):
    assert norm_gain.shape[0] == 1, "single-layer block"
    b, l, d = x.shape
    aw, sw = ATTN_WIDTH, SSM_WIDTH
    col_scale = jnp.where(jnp.arange(w_in.shape[-1]) < aw, LOG2E / math.sqrt(HEAD_DIM), 1.0)
    w_all = (w_in[0] * col_scale).astype(BF16)
    gain = norm_gain[0].reshape(1, d)

    m, ws, p, aa, ab = _ssm_chunk_params(
        ssm_a_re[0], ssm_a_im[0], ssm_log_dt[0], ssm_b_re[0], ssm_b_im[0], ssm_c_re[0], ssm_c_im[0])
    s, h = _ssm(x, w_all, [gain, m, ws, p, aa, ab, ssm_d[0].reshape(1, sw),
                        w_glu[0].astype(BF16), b_glu[0].reshape(1, 2 * sw)], steps=SSM_STEPS)

    consts = [gate_bias[0].reshape(1, 2 * d), _band_bias(rel_bias[0]),
              w_attn_out[0].astype(BF16), w_ssm_out[0].astype(BF16), w_out[0].astype(BF16),
              final_gain.reshape(1, d)]
    return _main(x, h, s, w_all, consts)
```
